```python
import math, functools
import jax, jax.numpy as jnp
from jax import lax
import numpy as np

D_MODEL = 2048
BATCH = 4
SEQ = 2048
DEPTH = 4
DEC_BATCH = 8
DEC_SEQ = 4
PAST_LEN = 16384
PAGE_SIZE = 128

HEAD_DIM = 128
W_A = 3 * D_MODEL // 8
N_HEADS_A = W_A // HEAD_DIM
GROUP_CH = 16
W_B = 3 * D_MODEL // 8
N_GROUPS = W_B // GROUP_CH
STATE_P = 64
N_HEADS_M = 4
W_M = D_MODEL // 4
HEAD_DIM_M = W_M // N_HEADS_M
N_MEM = 256
D_FF = ((8 * D_MODEL // 3 + 255) // 256) * 256
CONV_W = 3
N_BRANCH = 3
Q_BLOCK = 128
RMS_EPS = 1e-6
FORGET_BIAS = 3.0
IN_SIZES = (W_A, W_A, W_A, N_HEADS_A, W_B, W_M, N_BRANCH * D_MODEL)
W_IN = sum(IN_SIZES)

kernel_name = 'fox_s5_memxattn_convffn_hybrid_step'


def rmsnorm(x, g):
    xf = x.astype(jnp.float32)
    y = xf * lax.rsqrt(jnp.mean(xf * xf, axis=-1, keepdims=True) + RMS_EPS)
    return (y * g.astype(jnp.float32)).astype(x.dtype)


def split_in(h, w_in, b_f):
    z = jnp.einsum('bld,de->ble', h, w_in)
    B, L, _ = z.shape
    idx = np.cumsum(IN_SIZES)[:-1].tolist()
    q, k, v, fl, u, qm, gl = jnp.split(z, idx, axis=-1)
    q = q.reshape(B, L, N_HEADS_A, HEAD_DIM)
    k = k.reshape(B, L, N_HEADS_A, HEAD_DIM)
    v = v.reshape(B, L, N_HEADS_A, HEAD_DIM)
    logf = jax.nn.log_sigmoid(fl.astype(jnp.float32) + b_f.astype(jnp.float32))
    u = u.reshape(B, L, N_GROUPS, GROUP_CH)
    qm = qm.reshape(B, L, N_HEADS_M, HEAD_DIM_M)
    gates = jax.nn.sigmoid(gl.astype(jnp.float32)).reshape(B, L, N_BRANCH, D_MODEL).astype(h.dtype)
    return q, k, v, logf, u, qm, gates


def fox_prompt(q, k, v, logf):
    B, L, H, hd = q.shape
    scale = HEAD_DIM ** -0.5
    nb = L // Q_BLOCK
    cT = jnp.cumsum(logf, axis=1).transpose(0, 2, 1)
    qb = q.reshape(B, nb, Q_BLOCK, H, hd).transpose(1, 0, 2, 3, 4)
    cb = cT.reshape(B, H, nb, Q_BLOCK).transpose(2, 0, 1, 3)
    pos_k = jnp.arange(L)

    def one_block(args):
        qi, ci, i = args
        s = jnp.einsum('bqhd,bkhd->bhqk', qi, k).astype(jnp.float32) * scale
        s = s + ci[..., None] - cT[:, :, None, :]
        pos_q = i * Q_BLOCK + jnp.arange(Q_BLOCK)
        s = jnp.where(pos_k[None, :] <= pos_q[:, None], s, -jnp.inf)
        p = jax.nn.softmax(s, axis=-1)
        return jnp.einsum('bhqk,bkhd->bqhd', p.astype(v.dtype), v)

    o = lax.map(one_block, (qb, cb, jnp.arange(nb)))
    return o.transpose(1, 0, 2, 3, 4).reshape(B, L, H * hd)


def fox_sample(q, k_new, v_new, logf_new, *, k_past, v_past, logf_past):
    Bd, T = q.shape[:2]
    P = k_past.shape[1]
    scale = HEAD_DIM ** -0.5
    lf_p = logf_past.astype(jnp.float32)
    c_new = jnp.cumsum(logf_new, axis=1).transpose(0, 2, 1)
    c_past = (lf_p - lax.cumsum(lf_p, axis=1, reverse=True)).transpose(0, 2, 1)
    s_p = jnp.einsum('bqhd,bkhd->bhqk', q, k_past).astype(jnp.float32) * scale
    s_p = s_p + c_new[..., None] - c_past[:, :, None, :]
    s_n = jnp.einsum('bqhd,bkhd->bhqk', q, k_new).astype(jnp.float32) * scale
    s_n = s_n + c_new[..., None] - c_new[:, :, None, :]
    causal = jnp.arange(T)[None, :] <= jnp.arange(T)[:, None]
    s_n = jnp.where(causal, s_n, -jnp.inf)
    p = jax.nn.softmax(jnp.concatenate([s_p, s_n], axis=-1), axis=-1)
    o = (jnp.einsum('bhqk,bkhd->bqhd', p[..., :P].astype(v_past.dtype), v_past)
         + jnp.einsum('bhqk,bkhd->bqhd', p[..., P:].astype(v_new.dtype), v_new))
    return o.reshape(Bd, T, W_A)


def mem_kv(mem, g_mem, w_mkv):
    z = jnp.einsum('bmd,de->bme', rmsnorm(mem, g_mem), w_mkv)
    B, M, _ = z.shape
    k, v = jnp.split(z, 2, axis=-1)
    return k.reshape(B, M, N_HEADS_M, HEAD_DIM_M), v.reshape(B, M, N_HEADS_M, HEAD_DIM_M)


def mem_attn(qm, mk, mv):
    B, L = qm.shape[:2]
    s = jnp.einsum('blhd,bmhd->bhlm', qm, mk).astype(jnp.float32) * (HEAD_DIM_M ** -0.5)
    p = jax.nn.softmax(s, axis=-1)
    o = jnp.einsum('bhlm,bmhd->blhd', p.astype(mv.dtype), mv)
    return o.reshape(B, L, W_M)


def ssm_combine(e1, e2):
    a1r, a1i, b1r, b1i = e1
    a2r, a2i, b2r, b2i = e2
    ar = a1r * a2r - a1i * a2i
    ai = a1r * a2i + a1i * a2r
    br = a2r * b1r - a2i * b1i + b2r
    bi = a2r * b1i + a2i * b1r + b2i
    return ar, ai, br, bi


def s5_branch(u, h0_re, h0_im, a_re, a_im, log_dt, b_re, b_im, c_re, c_im, d_skip, w_glu):
    f32 = jnp.float32
    a_re = a_re.astype(f32)
    a_im = a_im.astype(f32)
    dt = jnp.exp(log_dt.astype(f32))[:, None]
    mag = jnp.exp(a_re * dt)
    ab_re = mag * jnp.cos(a_im * dt)
    ab_im = mag * jnp.sin(a_im * dt)
    den = a_re * a_re + a_im * a_im
    n_re = ab_re - 1.0
    q_re = (n_re * a_re + ab_im * a_im) / den
    q_im = (ab_im * a_re - n_re * a_im) / den
    b_re = b_re.astype(f32)
    b_im = b_im.astype(f32)
    bb_re = q_re[..., None] * b_re - q_im[..., None] * b_im
    bb_im = q_re[..., None] * b_im + q_im[..., None] * b_re
    uf = u.astype(f32)
    bu_re = jnp.einsum('blgc,gpc->blgp', uf, bb_re)
    bu_im = jnp.einsum('blgc,gpc->blgp', uf, bb_im)
    shp = bu_re.shape
    ac_re, ac_im, s_re, s_im = lax.associative_scan(
        ssm_combine, (jnp.broadcast_to(ab_re, shp), jnp.broadcast_to(ab_im, shp), bu_re, bu_im), axis=1)
    h0r = h0_re.astype(f32)[:, None]
    h0i = h0_im.astype(f32)[:, None]
    h_re = s_re + ac_re * h0r - ac_im * h0i
    h_im = s_im + ac_re * h0i + ac_im * h0r
    y = (jnp.einsum('blgp,gcp->blgc', h_re, c_re.astype(f32))
         - jnp.einsum('blgp,gcp->blgc', h_im, c_im.astype(f32))
         + d_skip.astype(f32).reshape(N_GROUPS, GROUP_CH) * uf)
    B, L = u.shape[:2]
    z = jax.nn.gelu(y.reshape(B, L, W_B))
    out = z * jax.nn.sigmoid(z @ w_glu.astype(f32))
    return out.astype(u.dtype), h_re[:, -1], h_im[:, -1]


def conv_ffn(h, conv_state, w_up, conv_w, conv_b, w_down):
    up = jnp.einsum('bld,df->blf', h, w_up)
    L = up.shape[1]
    ext = jnp.concatenate([conv_state.astype(up.dtype), up], axis=1)
    y = conv_b + conv_w[CONV_W - 1] * ext[:, CONV_W - 1:CONV_W - 1 + L]
    for j in range(CONV_W - 1):
        y = y + conv_w[j] * ext[:, j:j + L]
    a, b = jnp.split(y, 2, axis=-1)
    out = jnp.einsum('blf,fd->bld', jax.nn.gelu(a, approximate=True) * b, w_down)
    return out, ext[:, L:]


def hybrid_layer(x, p, fox_fn, mk, mv, h0_re, h0_im, conv_state):
    h = rmsnorm(x, p['g_pre_mix'])
    q, k, v, logf, u, qm, gates = split_in(h, p['w_in'], p['b_f'])
    o_a = fox_fn(q, k, v, logf)
    o_b, hr, hi = s5_branch(u, h0_re, h0_im, p['a_re'], p['a_im'], p['log_dt'], p['b_re'], p['b_im'],
                            p['c_re'], p['c_im'], p['d_skip'], p['w_glu'])
    o_m = mem_attn(qm, mk, mv)
    merged = (gates[:, :, 0] * (o_a @ p['p_a'])
              + gates[:, :, 1] * (o_b @ p['p_b'])
              + gates[:, :, 2] * (o_m @ p['p_m']))
    x = x + rmsnorm(merged @ p['w_out'], p['g_post_mix'])
    f, conv_new = conv_ffn(rmsnorm(x, p['g_pre_ffn']), conv_state, p['w_up'], p['conv_w'], p['conv_b'], p['w_down'])
    x = x + rmsnorm(f, p['g_post_ffn'])
    return x, k, v, logf, hr, hi, conv_new


def setup_inputs(seed: int = 0) -> dict:
    f32 = jnp.float32
    key = jax.random.key(seed)
    ks = iter(jax.random.split(key, 64))

    def nrm(shape, scale):
        return scale * jax.random.normal(next(ks), shape, f32)

    n_pages = PAST_LEN // PAGE_SIZE
    n_used = DEC_BATCH * n_pages
    n_pool = n_used + max(1, n_used // 4)
    x_prompt = nrm((BATCH, SEQ, D_MODEL), 1.0)
    x_sample = nrm((DEC_BATCH, DEC_SEQ, D_MODEL), 1.0)
    cache_k = nrm((DEPTH, n_pool, PAGE_SIZE, N_HEADS_A, HEAD_DIM), 1.0)
    cache_v = nrm((DEPTH, n_pool, PAGE_SIZE, N_HEADS_A, HEAD_DIM), 1.0)
    cache_logf = jax.nn.log_sigmoid(FORGET_BIAS + nrm((DEPTH, n_pool, PAGE_SIZE, N_HEADS_A), 1.0))
    cache_mem_k = nrm((DEPTH, DEC_BATCH, N_MEM, N_HEADS_M, HEAD_DIM_M), 1.0)
    cache_mem_v = nrm((DEPTH, DEC_BATCH, N_MEM, N_HEADS_M, HEAD_DIM_M), 1.0)
    state_ssm_re = nrm((DEPTH, DEC_BATCH, N_GROUPS, STATE_P), 0.1)
    state_ssm_im = nrm((DEPTH, DEC_BATCH, N_GROUPS, STATE_P), 0.1)
    state_conv = nrm((DEPTH, DEC_BATCH, CONV_W - 1, 2 * D_FF), 1.0)
    page_table = jax.random.permutation(next(ks), n_pool)[:n_used].reshape(DEC_BATCH, n_pages).astype(jnp.int32)
    mem_prompt = nrm((BATCH, N_MEM, D_MODEL), 1.0)
    w_in = nrm((DEPTH, D_MODEL, W_IN), D_MODEL ** -0.5)
    b_f = FORGET_BIAS + nrm((DEPTH, N_HEADS_A), 0.5)
    a_re = -0.5 + nrm((DEPTH, N_GROUPS, STATE_P), 0.01)
    a_im = jnp.pi * jnp.arange(STATE_P, dtype=f32) + nrm((DEPTH, N_GROUPS, STATE_P), 0.01)
    log_dt = jax.random.uniform(next(ks), (DEPTH, N_GROUPS), f32, math.log(1e-3), math.log(1e-1))
    b_re = nrm((DEPTH, N_GROUPS, STATE_P, GROUP_CH), (2 * GROUP_CH) ** -0.5)
    b_im = nrm((DEPTH, N_GROUPS, STATE_P, GROUP_CH), (2 * GROUP_CH) ** -0.5)
    c_re = nrm((DEPTH, N_GROUPS, GROUP_CH, STATE_P), 0.5)
    c_im = nrm((DEPTH, N_GROUPS, GROUP_CH, STATE_P), 0.5)
    d_skip = nrm((DEPTH, W_B), 1.0)
    w_glu = nrm((DEPTH, W_B, W_B), W_B ** -0.5)
    g_mem = 1.0 + nrm((DEPTH, D_MODEL), 0.1)
    w_mkv = nrm((DEPTH, D_MODEL, 2 * W_M), D_MODEL ** -0.5)
    p_a = nrm((DEPTH, W_A, D_MODEL), W_A ** -0.5)
    p_b = nrm((DEPTH, W_B, D_MODEL), W_B ** -0.5)
    p_m = nrm((DEPTH, W_M, D_MODEL), W_M ** -0.5)
    w_out = nrm((DEPTH, D_MODEL, D_MODEL), D_MODEL ** -0.5)
    g_pre_mix = 1.0 + nrm((DEPTH, D_MODEL), 0.1)
    g_post_mix = 1.0 + nrm((DEPTH, D_MODEL), 0.1)
    g_pre_ffn = 1.0 + nrm((DEPTH, D_MODEL), 0.1)
    g_post_ffn = 1.0 + nrm((DEPTH, D_MODEL), 0.1)
    w_up = nrm((DEPTH, D_MODEL, 2 * D_FF), D_MODEL ** -0.5)
    conv_w = nrm((DEPTH, CONV_W, 2 * D_FF), CONV_W ** -0.5)
    conv_b = nrm((DEPTH, 2 * D_FF), 0.01)
    w_down = nrm((DEPTH, D_FF, D_MODEL), D_FF ** -0.5)
    return {'x_prompt': x_prompt, 'x_sample': x_sample, 'cache_k': cache_k, 'cache_v': cache_v,
            'cache_logf': cache_logf, 'cache_mem_k': cache_mem_k, 'cache_mem_v': cache_mem_v,
            'state_ssm_re': state_ssm_re, 'state_ssm_im': state_ssm_im, 'state_conv': state_conv,
            'page_table': page_table, 'mem_prompt': mem_prompt,
            'w_in': w_in, 'b_f': b_f, 'a_re': a_re, 'a_im': a_im, 'log_dt': log_dt,
            'b_re': b_re, 'b_im': b_im, 'c_re': c_re, 'c_im': c_im, 'd_skip': d_skip, 'w_glu': w_glu,
            'g_mem': g_mem, 'w_mkv': w_mkv, 'p_a': p_a, 'p_b': p_b, 'p_m': p_m, 'w_out': w_out,
            'g_pre_mix': g_pre_mix, 'g_post_mix': g_post_mix, 'g_pre_ffn': g_pre_ffn, 'g_post_ffn': g_post_ffn,
            'w_up': w_up, 'conv_w': conv_w, 'conv_b': conv_b, 'w_down': w_down}


def reference(x_prompt, x_sample, cache_k, cache_v, cache_logf, cache_mem_k, cache_mem_v,
              state_ssm_re, state_ssm_im, state_conv, page_table, mem_prompt,
              w_in, b_f, a_re, a_im, log_dt, b_re, b_im, c_re, c_im, d_skip, w_glu,
              g_mem, w_mkv, p_a, p_b, p_m, w_out, g_pre_mix, g_post_mix, g_pre_ffn, g_post_ffn,
              w_up, conv_w, conv_b, w_down):
    f32 = jnp.float32
    Bp = x_prompt.shape[0]
    Bd = x_sample.shape[0]
    xp = x_prompt
    xs = x_sample
    pk, pv, plf, pmk, pmv, pre, pim, pcv = [], [], [], [], [], [], [], []
    sk, sv, slf, sre, sim, scv = [], [], [], [], [], []
    for l in range(DEPTH):
        p = {'w_in': w_in[l], 'b_f': b_f[l], 'a_re': a_re[l], 'a_im': a_im[l], 'log_dt': log_dt[l],
             'b_re': b_re[l], 'b_im': b_im[l], 'c_re': c_re[l], 'c_im': c_im[l], 'd_skip': d_skip[l],
             'w_glu': w_glu[l], 'p_a': p_a[l], 'p_b': p_b[l], 'p_m': p_m[l], 'w_out': w_out[l],
             'g_pre_mix': g_pre_mix[l], 'g_post_mix': g_post_mix[l], 'g_pre_ffn': g_pre_ffn[l],
             'g_post_ffn': g_post_ffn[l], 'w_up': w_up[l], 'conv_w': conv_w[l], 'conv_b': conv_b[l],
             'w_down': w_down[l]}
        mk, mv = mem_kv(mem_prompt, g_mem[l], w_mkv[l])
        h0 = jnp.zeros((Bp, N_GROUPS, STATE_P), f32)
        c0 = jnp.zeros((Bp, CONV_W - 1, 2 * D_FF), xp.dtype)
        xp, k, v, lf, hr, hi, cv = hybrid_layer(xp, p, fox_prompt, mk, mv, h0, h0, c0)
        pk.append(k); pv.append(v); plf.append(lf); pmk.append(mk); pmv.append(mv)
        pre.append(hr); pim.append(hi); pcv.append(cv)
        k_past = cache_k[l, page_table].reshape(Bd, -1, N_HEADS_A, HEAD_DIM)
        v_past = cache_v[l, page_table].reshape(Bd, -1, N_HEADS_A, HEAD_DIM)
        lf_past = cache_logf[l, page_table].reshape(Bd, -1, N_HEADS_A)
        fox_s = functools.partial(fox_sample, k_past=k_past, v_past=v_past, logf_past=lf_past)
        xs, k, v, lf, hr, hi, cv = hybrid_layer(xs, p, fox_s, cache_mem_k[l], cache_mem_v[l],
                                                state_ssm_re[l], state_ssm_im[l], state_conv[l])
        sk.append(k); sv.append(v); slf.append(lf); sre.append(hr); sim.append(hi); scv.append(cv)
    return (xp, xs,
            jnp.stack(pk), jnp.stack(pv), jnp.stack(plf), jnp.stack(pmk), jnp.stack(pmv),
            jnp.stack(pre), jnp.stack(pim), jnp.stack(pcv),
            jnp.stack(sk), jnp.stack(sv), jnp.stack(slf), jnp.stack(sre), jnp.stack(sim), jnp.stack(scv))
```

```python
import functools
import math

import jax
import jax.numpy as jnp
from jax import lax
from jax.experimental import pallas as pl
from jax.experimental.pallas import tpu as pltpu

F32 = jnp.float32
BF16 = jnp.bfloat16

RMS_EPS = 1e-6
HEAD_DIM = 128
GROUP_CH = 16
LANES = 128
SUBLANES = 8
MXU_DIM = 256
VMEM_LIMIT = 56 * 1024 * 1024
S5_GROUPS_PER_CHUNK = MXU_DIM // GROUP_CH
HEADS_PAD = SUBLANES


def _tile(n, pref):
    return pref if n % pref == 0 else n


def _params(sem, vmem=VMEM_LIMIT):
    return pltpu.CompilerParams(dimension_semantics=sem, vmem_limit_bytes=vmem)


def _resident(shape, index_map):
    return pl.BlockSpec(shape, index_map, pipeline_mode=pl.Buffered(1))


def _dot(a, b):
    return jnp.dot(a, b, preferred_element_type=F32)


def _dot_nt(a, b):
    return lax.dot_general(a, b, (((1,), (1,)), ((), ())), preferred_element_type=F32)


def _rms(x, g):
    return x * lax.rsqrt(jnp.mean(x * x, axis=-1, keepdims=True) + RMS_EPS) * g


def _sigmoid(x):
    return 1.0 / (1.0 + jnp.exp(-x))


def _log_sigmoid(x):
    return jnp.minimum(x, 0.0) - jnp.log1p(jnp.exp(-jnp.abs(x)))


def _gelu_tanh(x):
    c = math.sqrt(2.0 / math.pi)
    return 0.5 * x * (1.0 + jnp.tanh(c * (x + 0.044715 * (x * x * x))))


def _rmsnorm_kernel(x_ref, g_ref, o_ref):
    o_ref[...] = _rms(x_ref[...], g_ref[...]).astype(o_ref.dtype)


def rmsnorm_bf16(x, g):
    M, D = x.shape
    bm = _tile(M, 512)
    return pl.pallas_call(
        _rmsnorm_kernel, grid=(M // bm,),
        in_specs=[pl.BlockSpec((bm, D), lambda i: (i, 0)), pl.BlockSpec((1, D), lambda i: (0, 0))],
        out_specs=pl.BlockSpec((bm, D), lambda i: (i, 0)),
        out_shape=jax.ShapeDtypeStruct((M, D), BF16),
        compiler_params=_params(("parallel",)), name="rmsnorm",
    )(x, g.reshape(1, D))


def _inproj_kernel(h_ref, w_ref, bf_ref, q_ref, k_ref, v_ref, kb_ref, vb_ref, u_ref, qm_ref, lf_ref,
                   *, wa, wb, wm, nh):
    h = h_ref[...]
    scale = HEAD_DIM ** -0.5

    def proj(a, b):
        return _dot(h, w_ref[:, a:b])

    q_ref[...] = (proj(0, wa) * scale).astype(BF16)
    k = proj(wa, 2 * wa)
    k_ref[...] = k
    kb_ref[...] = k.astype(BF16)
    v = proj(2 * wa, 3 * wa)
    v_ref[...] = v
    vb_ref[...] = v.astype(BF16)
    o = 3 * wa
    u_ref[...] = proj(o, o + wb)
    o += wb
    qm_ref[...] = (proj(o, o + wm) * scale).astype(BF16)
    o += wm
    fl = proj(o, o + LANES) + bf_ref[...]
    lf_ref[...] = _log_sigmoid(fl)[:, :nh]


def inproj(h, w, bf, wa, wb, wm, nh):
    M, D = h.shape
    NW = w.shape[1]
    bm = _tile(M, 512)
    row = lambda n: pl.BlockSpec((bm, n), lambda i: (i, 0))
    outs = [(wa, BF16), (wa, F32), (wa, F32), (wa, BF16), (wa, BF16), (wb, F32), (wm, BF16), (nh, F32)]
    return pl.pallas_call(
        functools.partial(_inproj_kernel, wa=wa, wb=wb, wm=wm, nh=nh), grid=(M // bm,),
        in_specs=[row(D), _resident((D, NW), lambda i: (0, 0)), pl.BlockSpec((1, LANES), lambda i: (0, 0))],
        out_specs=[row(n) for n, _ in outs],
        out_shape=[jax.ShapeDtypeStruct((M, n), dt) for n, dt in outs],
        compiler_params=_params(("parallel",)), name="inproj",
    )(h, w, bf)


def _cumsum_kernel(x_ref, o_ref):
    x = x_ref[0]
    L = x.shape[-1]
    lane = lax.broadcasted_iota(jnp.int32, x.shape, 1)
    s = 1
    while s < L:
        x = x + jnp.where(lane >= s, pltpu.roll(x, s, axis=1), 0.0)
        s *= 2
    o_ref[0] = x


def cumsum_lanes(x):
    B, H, L = x.shape
    return pl.pallas_call(
        _cumsum_kernel, grid=(B,),
        in_specs=[pl.BlockSpec((1, H, L), lambda b: (b, 0, 0))],
        out_specs=pl.BlockSpec((1, H, L), lambda b: (b, 0, 0)),
        out_shape=jax.ShapeDtypeStruct((B, H, L), F32),
        compiler_params=_params(("parallel",)), name="cumsum",
    )(x)


def _fox_kernel(q_ref, k_ref, v_ref, cq_ref, ck_ref, o_ref, m_sc, l_sc, acc_sc):
    qi = pl.program_id(2)
    ki = pl.program_id(3)

    @pl.when(ki == 0)
    def _():
        m_sc[...] = jnp.full(m_sc.shape, -jnp.inf, F32)
        l_sc[...] = jnp.zeros(l_sc.shape, F32)
        acc_sc[...] = jnp.zeros(acc_sc.shape, F32)

    def update(masked):
        s = _dot_nt(q_ref[...], k_ref[...]) + cq_ref[...] - ck_ref[...]
        if masked:
            row = lax.broadcasted_iota(jnp.int32, s.shape, 0)
            col = lax.broadcasted_iota(jnp.int32, s.shape, 1)
            s = jnp.where(col <= row, s, -jnp.inf)
        m_prev = m_sc[...]
        m_new = jnp.maximum(m_prev, jnp.max(s, axis=-1, keepdims=True))
        alpha = jnp.exp(m_prev - m_new)
        p = jnp.exp(s - m_new)
        l_sc[...] = alpha * l_sc[...] + jnp.sum(p, axis=-1, keepdims=True)
        acc_sc[...] = alpha * acc_sc[...] + _dot(p.astype(BF16), v_ref[...])
        m_sc[...] = m_new

    @pl.when(ki < qi)
    def _():
        update(False)

    @pl.when(ki == qi)
    def _():
        update(True)
        o_ref[...] = (acc_sc[...] / l_sc[...]).astype(o_ref.dtype)


def fox_prompt(q, k, v, cq, ck, nh):
    B, L, W = q.shape
    bq = _tile(L, 512)
    nq = L // bq
    qspec = pl.BlockSpec((None, bq, HEAD_DIM), lambda b, h, qi, ki: (b, qi, h))
    kspec = pl.BlockSpec((None, bq, HEAD_DIM), lambda b, h, qi, ki: (b, jnp.minimum(ki, qi), h))
    return pl.pallas_call(
        _fox_kernel, grid=(B, nh, nq, nq),
        in_specs=[qspec, kspec, kspec,
                  pl.BlockSpec((None, None, bq, 1), lambda b, h, qi, ki: (b, h, qi, 0)),
                  pl.BlockSpec((None, None, 1, bq), lambda b, h, qi, ki: (b, h, 0, jnp.minimum(ki, qi)))],
        out_specs=qspec,
        out_shape=jax.ShapeDtypeStruct((B, L, W), BF16),
        scratch_shapes=[pltpu.VMEM((bq, 1), F32), pltpu.VMEM((bq, 1), F32), pltpu.VMEM((bq, HEAD_DIM), F32)],
        compiler_params=_params(("parallel", "parallel", "parallel", "arbitrary")), name="fox_prompt",
    )(q, k, v, cq, ck)


def _fox_decode_kernel(pt_ref, q_ref, lx_ref, kn_ref, vn_ref, *refs, pps, n_new):
    kp = refs[:pps]
    vp = refs[pps:2 * pps]
    lp = refs[2 * pps:3 * pps]
    o_ref, m_sc, l_sc, acc_sc, suf_sc = refs[3 * pps:]
    del pt_ref
    step = pl.program_id(1)
    q = q_ref[...]
    R = q.shape[0]
    reps = R // HEADS_PAD

    @pl.when(step == 0)
    def _():
        m_sc[...] = jnp.full(m_sc.shape, -jnp.inf, F32)
        l_sc[...] = jnp.zeros(l_sc.shape, F32)
        acc_sc[...] = jnp.zeros(acc_sc.shape, F32)
        suf_sc[...] = jnp.zeros(suf_sc.shape, F32)

    ck = lx_ref[...]
    col = lax.broadcasted_iota(jnp.int32, ck.shape, 1)
    row_t = lax.broadcasted_iota(jnp.int32, ck.shape, 0) // HEADS_PAD
    s = 1
    while s < n_new:
        ck = ck + jnp.where(col >= s, pltpu.roll(ck, s, axis=1), 0.0)
        s *= 2
    cq = jnp.sum(jnp.where(col == row_t, ck, 0.0), axis=-1, keepdims=True)

    def update(sc, v):
        m_prev = m_sc[...]
        m_new = jnp.maximum(m_prev, jnp.max(sc, axis=-1, keepdims=True))
        alpha = jnp.exp(m_prev - m_new)
        p = jnp.exp(sc - m_new)
        l_sc[...] = alpha * l_sc[...] + jnp.sum(p, axis=-1, keepdims=True)
        acc_sc[...] = alpha * acc_sc[...] + _dot(p.astype(BF16), v)
        m_sc[...] = m_new

    lane = lax.broadcasted_iota(jnp.int32, (HEADS_PAD, LANES), 1)
    for i in range(pps):
        lf = lp[i][...]
        inc = lf
        s = 1
        while s < LANES:
            inc = inc + jnp.where(lane < LANES - s, pltpu.roll(inc, LANES - s, axis=1), 0.0)
            s *= 2
        after = suf_sc[...]
        bias = inc - lf + after
        suf_sc[...] = after + jnp.broadcast_to(inc[:, 0:1], after.shape)
        bias = jnp.concatenate([bias] * reps, axis=0)
        sc = _dot_nt(q, kp[i][...].astype(BF16)) + cq + bias
        update(sc, vp[i][...].astype(BF16))

    @pl.when(step == pl.num_programs(1) - 1)
    def _():
        sc = _dot_nt(q, kn_ref[...]) + cq - ck
        sc = jnp.where((col <= row_t) & (col < n_new), sc, -jnp.inf)
        update(sc, vn_ref[...])
        o_ref[...] = acc_sc[...] / l_sc[...]


def fox_decode(layer, page_table, qblk, lx, kn, vn, cache_k, cache_v, cache_lf, n_new):
    Bd, R, W = qblk.shape
    n_pages = page_table.shape[1]
    page = cache_k.shape[2]
    pps = 8 if n_pages % 8 == 0 else 1
    steps = n_pages // pps

    def page_spec(i, shape):
        return pl.BlockSpec((None, None) + shape,
                            lambda b, s, pt: (layer, pt[b, n_pages - 1 - (s * pps + i)], 0, 0))

    per_b = lambda shape: pl.BlockSpec((None,) + shape, lambda b, s, pt: (b, 0, 0))
    in_specs = ([per_b((R, W)), per_b((R, LANES)), per_b((page, W)), per_b((page, W))]
                + [page_spec(i, (page, W)) for i in range(pps)]
                + [page_spec(i, (page, W)) for i in range(pps)]
                + [page_spec(i, (HEADS_PAD, page)) for i in range(pps)])
    grid_spec = pltpu.PrefetchScalarGridSpec(
        num_scalar_prefetch=1, grid=(Bd, steps), in_specs=in_specs,
        out_specs=per_b((R, W)),
        scratch_shapes=[pltpu.VMEM((R, 1), F32), pltpu.VMEM((R, 1), F32), pltpu.VMEM((R, W), F32),
                        pltpu.VMEM((HEADS_PAD, LANES), F32)])
    return pl.pallas_call(
        functools.partial(_fox_decode_kernel, pps=pps, n_new=n_new), grid_spec=grid_spec,
        out_shape=jax.ShapeDtypeStruct((Bd, R, W), F32),
        compiler_params=_params(("parallel", "arbitrary")), name="fox_decode",
    )(page_table, qblk, lx, kn, vn, *([cache_k] * pps), *([cache_v] * pps), *([cache_lf] * pps))


def _s5_disc_kernel(are_ref, aim_ref, ldt_ref, bre_ref, bim_ref, abre_ref, abim_ref, bbre_ref, bbim_ref):
    a_re = are_ref[0]
    a_im = aim_ref[0]
    dt = jnp.exp(ldt_ref[0])
    mag = jnp.exp(a_re * dt)
    ab_re = mag * jnp.cos(a_im * dt)
    ab_im = mag * jnp.sin(a_im * dt)
    den = a_re * a_re + a_im * a_im
    n_re = ab_re - 1.0
    q_re = (n_re * a_re + ab_im * a_im) / den
    q_im = (ab_im * a_re - n_re * a_im) / den
    b_re = bre_ref[0]
    b_im = bim_ref[0]
    abre_ref[0] = ab_re
    abim_ref[0] = ab_im
    bbre_ref[0] = q_re * b_re - q_im * b_im
    bbim_ref[0] = q_re * b_im + q_im * b_re


def s5_discretize(a_re, a_im, log_dt, b_re_t, b_im_t):
    Dp, G, P = a_re.shape
    C = b_re_t.shape[2]
    gp = pl.BlockSpec((1, G, 1, P), lambda l: (l, 0, 0, 0))
    gcp = pl.BlockSpec((1, G, C, P), lambda l: (l, 0, 0, 0))
    return pl.pallas_call(
        _s5_disc_kernel, grid=(Dp,),
        in_specs=[gp, gp, pl.BlockSpec((1, G, 1, 1), lambda l: (l, 0, 0, 0)), gcp, gcp],
        out_specs=[gp, gp, gcp, gcp],
        out_shape=[jax.ShapeDtypeStruct((Dp, G, 1, P), F32)] * 2 + [jax.ShapeDtypeStruct((Dp, G, C, P), F32)] * 2,
        compiler_params=_params(("parallel",)), name="s5_discretize",
    )(a_re.reshape(Dp, G, 1, P), a_im.reshape(Dp, G, 1, P), log_dt.reshape(Dp, G, 1, 1), b_re_t, b_im_t)


def _s5_kernel(u_ref, h0re_ref, h0im_ref, are_ref, aim_ref, wbu_ref, wcre_ref, wcim_ref, d_ref, wglu_ref,
               o_ref, hre_ref, him_ref, bre_sc, bim_sc, *, nb, lane_chunk):
    n = pl.program_id(0)
    rows, wb = u_ref.shape
    tc = rows // nb
    gp = bre_sc.shape[1]
    kc = wbu_ref.shape[1]
    sc = wbu_ref.shape[2] // 2
    n_chunks = wbu_ref.shape[0]

    @pl.when(n == 0)
    def _():
        hre_ref[...] = h0re_ref[...]
        him_ref[...] = h0im_ref[...]

    u = u_ref[...]
    ub = u.astype(BF16)
    for c in range(n_chunks):
        bu = _dot(ub[:, c * kc:(c + 1) * kc], wbu_ref[c])
        bre_sc[:, c * sc:(c + 1) * sc] = bu[:, :sc]
        bim_sc[:, c * sc:(c + 1) * sc] = bu[:, sc:]

    spt = SUBLANES // nb
    for c in range(gp // lane_chunk):
        cs = slice(c * lane_chunk, (c + 1) * lane_chunk)
        ar = jnp.broadcast_to(are_ref[:, cs], (SUBLANES, lane_chunk))
        ai = jnp.broadcast_to(aim_ref[:, cs], (SUBLANES, lane_chunk))
        band = lax.broadcasted_iota(jnp.int32, (SUBLANES, lane_chunk), 0) // nb

        def step(t, carry, cs=cs, ar=ar, ai=ai, band=band):
            hr, hi = carry
            r0 = pl.multiple_of(t * SUBLANES, SUBLANES)
            xr = bre_sc[pl.ds(r0, SUBLANES), cs]
            xi = bim_sc[pl.ds(r0, SUBLANES), cs]
            out_r = out_i = None
            for k in range(spt):
                hr, hi = ar * hr - ai * hi + xr, ar * hi + ai * hr + xi
                out_r = hr if k == 0 else jnp.where(band == k, hr, out_r)
                out_i = hi if k == 0 else jnp.where(band == k, hi, out_i)
                if spt > 1:
                    hr = pltpu.roll(hr, nb, axis=0)
                    hi = pltpu.roll(hi, nb, axis=0)
            bre_sc[pl.ds(r0, SUBLANES), cs] = out_r
            bim_sc[pl.ds(r0, SUBLANES), cs] = out_i
            return hr, hi

        init = lambda ref: jnp.concatenate([ref[:, cs]] * spt, axis=0)
        hr, hi = lax.fori_loop(0, rows // SUBLANES, step, (init(hre_ref), init(him_ref)))
        hre_ref[:, cs] = hr[:nb]
        him_ref[:, cs] = hi[:nb]

    ys = []
    for c in range(n_chunks):
        hr = bre_sc[:, c * sc:(c + 1) * sc].astype(BF16)
        hi = bim_sc[:, c * sc:(c + 1) * sc].astype(BF16)
        ys.append(_dot(hr, wcre_ref[c]) - _dot(hi, wcim_ref[c]))
    y = jnp.concatenate(ys, axis=1) + d_ref[...] * u
    z = _gelu_tanh(y)
    o_ref[...] = (z * _sigmoid(_dot(z.astype(BF16), wglu_ref[...]))).astype(o_ref.dtype)


def s5_branch(u_tb, h0_re, h0_im, ab_re, ab_im, wbu, wc_re, wc_im, d_skip, w_glu, nb):
    rows_all, wb = u_tb.shape
    L = rows_all // nb
    gp = h0_re.shape[1]
    tc = _tile(L, 128)
    rows = tc * nb
    lane_chunk = _tile(gp, 768)
    full = lambda a: _resident(a.shape, lambda n: (0,) * a.ndim)
    st = pl.BlockSpec((nb, gp), lambda n: (0, 0))
    return pl.pallas_call(
        functools.partial(_s5_kernel, nb=nb, lane_chunk=lane_chunk), grid=(L // tc,),
        in_specs=[pl.BlockSpec((rows, wb), lambda n: (n, 0)), st, st, full(ab_re), full(ab_im),
                  full(wbu), full(wc_re), full(wc_im), full(d_skip), full(w_glu)],
        out_specs=[pl.BlockSpec((rows, wb), lambda n: (n, 0)), st, st],
        out_shape=[jax.ShapeDtypeStruct((rows_all, wb), BF16), jax.ShapeDtypeStruct((nb, gp), F32),
                   jax.ShapeDtypeStruct((nb, gp), F32)],
        scratch_shapes=[pltpu.VMEM((rows, gp), F32), pltpu.VMEM((rows, gp), F32)],
        compiler_params=_params(("arbitrary",)), name="s5_branch",
    )(u_tb, h0_re, h0_im, ab_re, ab_im, wbu, wc_re, wc_im, d_skip, w_glu)


def _mem_kv_kernel(x_ref, g_ref, w_ref, k_ref, v_ref, kb_ref, vb_ref):
    h = _rms(x_ref[...], g_ref[0]).astype(BF16)
    z = _dot(h, w_ref[0])
    wm = k_ref.shape[-1]
    k_ref[0] = z[:, :wm]
    v_ref[0] = z[:, wm:]
    kb_ref[0] = z[:, :wm].astype(BF16)
    vb_ref[0] = z[:, wm:].astype(BF16)


def mem_kv(mem, g_mem, w_mkv):
    R, D = mem.shape
    Dp, _, W2 = w_mkv.shape
    wm = W2 // 2
    bm = _tile(R, 512)
    o = pl.BlockSpec((1, bm, wm), lambda l, i: (l, i, 0))
    return pl.pallas_call(
        _mem_kv_kernel, grid=(Dp, R // bm),
        in_specs=[pl.BlockSpec((bm, D), lambda l, i: (i, 0)), pl.BlockSpec((1, 1, D), lambda l, i: (l, 0, 0)),
                  pl.BlockSpec((1, D, W2), lambda l, i: (l, 0, 0))],
        out_specs=[o, o, o, o],
        out_shape=[jax.ShapeDtypeStruct((Dp, R, wm), F32)] * 2 + [jax.ShapeDtypeStruct((Dp, R, wm), BF16)] * 2,
        compiler_params=_params(("parallel", "arbitrary")), name="mem_kv",
    )(mem, g_mem, w_mkv)


def _mem_attn_kernel(q_ref, k_ref, v_ref, o_ref):
    nh = q_ref.shape[-1] // HEAD_DIM
    for h in range(nh):
        hs = slice(h * HEAD_DIM, (h + 1) * HEAD_DIM)
        s = _dot_nt(q_ref[:, hs], k_ref[:, hs])
        p = jnp.exp(s - jnp.max(s, axis=-1, keepdims=True))
        p = p / jnp.sum(p, axis=-1, keepdims=True)
        o_ref[:, hs] = _dot(p.astype(BF16), v_ref[:, hs]).astype(o_ref.dtype)


def mem_attn(q, k, v):
    B, L, W = q.shape
    Nm = k.shape[1]
    bq = _tile(L, 512)
    qs = pl.BlockSpec((None, bq, W), lambda b, i: (b, i, 0))
    ks = pl.BlockSpec((None, Nm, W), lambda b, i: (b, 0, 0))
    return pl.pallas_call(
        _mem_attn_kernel, grid=(B, L // bq), in_specs=[qs, ks, ks], out_specs=qs,
        out_shape=jax.ShapeDtypeStruct((B, L, W), BF16),
        compiler_params=_params(("parallel", "parallel")), name="mem_attn",
    )(q, k, v)


def _merge_kernel(h_ref, oa_ref, ob_ref, om_ref, g0_ref, g1_ref, g2_ref, pa_ref, pb_ref, pm_ref, o_ref):
    h = h_ref[...]
    acc = _sigmoid(_dot(h, g0_ref[...])) * _dot(oa_ref[...], pa_ref[...])
    acc += _sigmoid(_dot(h, g1_ref[...])) * _dot(ob_ref[...], pb_ref[...])
    acc += _sigmoid(_dot(h, g2_ref[...])) * _dot(om_ref[...], pm_ref[...])
    o_ref[...] = acc.astype(o_ref.dtype)


def merge(h, oa, ob, om, w_gate, pa, pb, pm):
    M, D = h.shape
    bm = _tile(M, 1024)
    bn = _tile(D, 512)
    nj = D // bn
    rowspec = lambda a: pl.BlockSpec((bm, a.shape[1]), lambda j, i: (i, 0))
    colspec = lambda a: pl.BlockSpec((a.shape[0], bn), lambda j, i: (0, j))
    gate = lambda b: pl.BlockSpec((D, bn), lambda j, i: (0, b * nj + j))
    return pl.pallas_call(
        _merge_kernel, grid=(nj, M // bm),
        in_specs=[rowspec(h), rowspec(oa), rowspec(ob), rowspec(om), gate(0), gate(1), gate(2),
                  colspec(pa), colspec(pb), colspec(pm)],
        out_specs=pl.BlockSpec((bm, bn), lambda j, i: (i, j)),
        out_shape=jax.ShapeDtypeStruct((M, D), BF16),
        compiler_params=_params(("parallel", "arbitrary")), name="merge",
    )(h, oa, ob, om, w_gate, w_gate, w_gate, pa, pb, pm)


def _proj_res_kernel(a_ref, w_ref, x_ref, gpost_ref, gnext_ref, xo_ref, *h_ref):
    f = _dot(a_ref[...], w_ref[...])
    x = x_ref[...] + _rms(f, gpost_ref[...])
    xo_ref[...] = x
    if h_ref:
        h_ref[0][...] = _rms(x, gnext_ref[...]).astype(BF16)


def proj_residual(a, w, x, g_post, g_next, emit_h=True):
    M, K = a.shape
    D = w.shape[1]
    bm = _tile(M, 256)
    row = lambda n: pl.BlockSpec((bm, n), lambda i: (i, 0))
    g = pl.BlockSpec((1, D), lambda i: (0, 0))
    out_specs = [row(D)] + ([row(D)] if emit_h else [])
    out_shape = [jax.ShapeDtypeStruct((M, D), F32)] + ([jax.ShapeDtypeStruct((M, D), BF16)] if emit_h else [])
    res = pl.pallas_call(
        _proj_res_kernel, grid=(M // bm,),
        in_specs=[row(K), _resident((K, D), lambda i: (0, 0)), row(D), g, g],
        out_specs=out_specs, out_shape=out_shape,
        compiler_params=_params(("parallel",)), name="proj_residual",
    )(a, w, x, g_post.reshape(1, D), g_next.reshape(1, D))
    return (res[0], res[1]) if emit_h else (res[0], None)


def _ffn_up_kernel(h_ref, wa_ref, wb_ref, cwa_ref, cwb_ref, cba_ref, cbb_ref, act_ref, sa_ref, sb_ref,
                   ca_sc, cb_sc, *, tiles_per_seq):
    i = pl.program_id(1)
    h = h_ref[...]
    bm = h.shape[0]

    @pl.when(i % tiles_per_seq == 0)
    def _():
        ca_sc[...] = jnp.zeros(ca_sc.shape, F32)
        cb_sc[...] = jnp.zeros(cb_sc.shape, F32)

    def conv(w_ref, c_sc, cw_ref, cb_ref, s_ref):
        u = _dot(h, w_ref[...])
        prev = c_sc[...]
        row = lax.broadcasted_iota(jnp.int32, u.shape, 0)
        u1 = jnp.where(row == 0, prev[SUBLANES - 1:SUBLANES], pltpu.roll(u, 1, axis=0))
        u2 = jnp.where(row == 0, prev[SUBLANES - 2:SUBLANES - 1],
                       jnp.where(row == 1, prev[SUBLANES - 1:SUBLANES], pltpu.roll(u, 2, axis=0)))
        c_sc[...] = u[bm - SUBLANES:]
        s_ref[...] = u[bm - 2:]
        return cb_ref[...] + cw_ref[2:3] * u + cw_ref[1:2] * u1 + cw_ref[0:1] * u2

    ya = conv(wa_ref, ca_sc, cwa_ref, cba_ref, sa_ref)
    yb = conv(wb_ref, cb_sc, cwb_ref, cbb_ref, sb_ref)
    act_ref[...] = (_gelu_tanh(ya) * yb).astype(act_ref.dtype)


def ffn_up(h, w_up, conv_w, conv_b, n_seq):
    M, D = h.shape
    F2 = w_up.shape[1]
    F = F2 // 2
    L = M // n_seq
    bm = _tile(L, 1024)
    bn = _tile(F, 512)
    nj = F // bn
    tps = L // bm
    a_col = lambda r: pl.BlockSpec((r, bn), lambda j, i: (0, j))
    b_col = lambda r: pl.BlockSpec((r, bn), lambda j, i: (0, nj + j))
    st = pl.BlockSpec((None, 2, bn), lambda j, i: (i // tps, 0, j))
    return pl.pallas_call(
        functools.partial(_ffn_up_kernel, tiles_per_seq=tps), grid=(nj, M // bm),
        in_specs=[pl.BlockSpec((bm, D), lambda j, i: (i, 0)), a_col(D), b_col(D), a_col(3), b_col(3),
                  a_col(1), b_col(1)],
        out_specs=[pl.BlockSpec((bm, bn), lambda j, i: (i, j)), st, st],
        out_shape=[jax.ShapeDtypeStruct((M, F), BF16), jax.ShapeDtypeStruct((n_seq, 2, F), F32),
                   jax.ShapeDtypeStruct((n_seq, 2, F), F32)],
        scratch_shapes=[pltpu.VMEM((SUBLANES, bn), F32), pltpu.VMEM((SUBLANES, bn), F32)],
        compiler_params=_params(("parallel", "arbitrary")), name="ffn_up",
    )(h, w_up, w_up, conv_w, conv_w, conv_b, conv_b)


def _ffn_up_step_kernel(h_ref, wa_ref, wb_ref, sta_ref, stb_ref, cwa_ref, cwb_ref, cba_ref, cbb_ref,
                        act_ref, sa_ref, sb_ref, *, n_seq):
    h = h_ref[...]
    T = h.shape[0] // n_seq

    def conv(w_ref, st_ref, cw_ref, cb_ref, s_ref):
        u = _dot(h, w_ref[...])
        ext = jnp.concatenate([st_ref[...], u], axis=0)
        s_ref[...] = ext[T * n_seq:]
        return (cb_ref[...] + cw_ref[2:3] * u + cw_ref[1:2] * ext[n_seq:(T + 1) * n_seq]
                + cw_ref[0:1] * ext[:T * n_seq])

    ya = conv(wa_ref, sta_ref, cwa_ref, cba_ref, sa_ref)
    yb = conv(wb_ref, stb_ref, cwb_ref, cbb_ref, sb_ref)
    act_ref[...] = (_gelu_tanh(ya) * yb).astype(act_ref.dtype)


def ffn_up_step(h_tb, w_up, conv_w, conv_b, state_tb, n_seq):
    M, D = h_tb.shape
    F2 = w_up.shape[1]
    F = F2 // 2
    bn = _tile(F, 512)
    nj = F // bn
    a_col = lambda r: pl.BlockSpec((r, bn), lambda j: (0, j))
    b_col = lambda r: pl.BlockSpec((r, bn), lambda j: (0, nj + j))
    return pl.pallas_call(
        functools.partial(_ffn_up_step_kernel, n_seq=n_seq), grid=(nj,),
        in_specs=[pl.BlockSpec((M, D), lambda j: (0, 0)), a_col(D), b_col(D), a_col(2 * n_seq), b_col(2 * n_seq),
                  a_col(3), b_col(3), a_col(1), b_col(1)],
        out_specs=[a_col(M), a_col(2 * n_seq), a_col(2 * n_seq)],
        out_shape=[jax.ShapeDtypeStruct((M, F), BF16), jax.ShapeDtypeStruct((2 * n_seq, F), F32),
                   jax.ShapeDtypeStruct((2 * n_seq, F), F32)],
        compiler_params=_params(("parallel",)), name="ffn_up_step",
    )(h_tb, w_up, w_up, state_tb, state_tb, conv_w, conv_w, conv_b, conv_b)


def _s5_weights(bb_re_t, bb_im_t, c_re, c_im):
    G, C, P = bb_re_t.shape
    gc = S5_GROUPS_PER_CHUNK if G % S5_GROUPS_PER_CHUNK == 0 else G
    n = G // gc
    eye = jnp.eye(gc, dtype=F32)

    def bu(bb):
        return jnp.einsum('ngcp,gh->ngchp', bb.reshape(n, gc, C, P), eye).reshape(n, gc * C, gc * P)

    def cy(c):
        return jnp.einsum('ngcp,gh->ngphc', c.reshape(n, gc, C, P), eye).reshape(n, gc * P, gc * C)

    wbu = jnp.concatenate([bu(bb_re_t), bu(bb_im_t)], axis=-1).astype(BF16)
    return wbu, cy(c_re).astype(BF16), cy(c_im).astype(BF16)


def _layer_tokens(x, h, wl, branch_fn, ffn_fn, emit_h):
    q, k, v, kb, vb, u, qm, lf = inproj(h, wl['w_proj'], wl['b_f'], wl['wa'], wl['wb'], wl['wm'], wl['nh'])
    o_a, o_b, o_m, extra = branch_fn(q, kb, vb, lf, u, qm)
    merged = merge(h, o_a, o_b, o_m, wl['w_gate'], wl['p_a'], wl['p_b'], wl['p_m'])
    x, h2 = proj_residual(merged, wl['w_out'], x, wl['g_post_mix'], wl['g_pre_ffn'])
    act, conv_new = ffn_fn(h2)
    x, h_next = proj_residual(act, wl['w_down'], x, wl['g_post_ffn'], wl['g_next'], emit_h=emit_h)
    return x, h_next, k, v, lf, conv_new, extra


def kernel(x_prompt, x_sample, cache_k, cache_v, cache_logf, cache_mem_k, cache_mem_v, state_ssm_re, state_ssm_im, state_conv, page_table, mem_prompt, w_in, b_f, a_re, a_im, log_dt, b_re, b_im, c_re, c_im, d_skip, w_glu, g_mem, w_mkv, p_a, p_b, p_m, w_out, g_pre_mix, g_post_mix, g_pre_ffn, g_post_ffn, w_up, conv_w, conv_b, w_down):
    Bp, L, D = x_prompt.shape
    Bd, T, _ = x_sample.shape
    depth = w_in.shape[0]
    wa = p_a.shape[1]
    wb = p_b.shape[1]
    wm = p_m.shape[1]
    nh = b_f.shape[1]
    nhm = wm // HEAD_DIM
    G, P = a_re.shape[1:]
    C = b_re.shape[3]
    gp = G * P
    n_mem = mem_prompt.shape[1]
    n_pool, page = cache_k.shape[1:3]
    F2 = w_up.shape[2]

    o_fl = 3 * wa
    o_u = o_fl + nh
    o_qm = o_u + wb
    o_g = o_qm + wm
    w_fl = jnp.pad(w_in[:, :, o_fl:o_u], ((0, 0), (0, 0), (0, LANES - nh)))
    w_proj = jnp.concatenate([w_in[:, :, :o_fl], w_in[:, :, o_u:o_g], w_fl], axis=-1).astype(BF16)
    w_gate = w_in[:, :, o_g:].astype(BF16)
    b_f_pad = jnp.pad(b_f, ((0, 0), (0, LANES - nh))).reshape(depth, 1, LANES)
    ab_re, ab_im, bb_re_t, bb_im_t = s5_discretize(a_re, a_im, log_dt, b_re.transpose(0, 1, 3, 2),
                                                   b_im.transpose(0, 1, 3, 2))
    bf = lambda a: a.astype(BF16)
    w_mkv_b, p_a_b, p_b_b, p_m_b, w_out_b, w_up_b, w_down_b, w_glu_b = map(
        bf, (w_mkv, p_a, p_b, p_m, w_out, w_up, w_down, w_glu))

    mk, mv, mkb, mvb = mem_kv(mem_prompt.reshape(Bp * n_mem, D), g_mem.reshape(depth, 1, D), w_mkv_b)

    cache_k2 = cache_k.reshape(depth, n_pool, page, wa)
    cache_v2 = cache_v.reshape(depth, n_pool, page, wa)
    cache_lf_t = jnp.pad(cache_logf.transpose(0, 1, 3, 2), ((0, 0), (0, 0), (0, HEADS_PAD - nh), (0, 0)))
    cache_mk_b = bf(cache_mem_k).reshape(depth, Bd, n_mem, wm)
    cache_mv_b = bf(cache_mem_v).reshape(depth, Bd, n_mem, wm)

    xp = x_prompt.reshape(Bp * L, D)
    xs = x_sample.reshape(Bd * T, D)
    hp = rmsnorm_bf16(xp, g_pre_mix[0])
    hs = rmsnorm_bf16(xs, g_pre_mix[0])
    zeros_p = jnp.zeros((Bp, gp), F32)
    head_ids = jnp.arange(nh)

    outs = [[] for _ in range(14)]
    for l in range(depth):
        wbu, wc_re, wc_im = _s5_weights(bb_re_t[l], bb_im_t[l], c_re[l], c_im[l])
        wl = dict(w_proj=w_proj[l], b_f=b_f_pad[l], wa=wa, wb=wb, wm=wm, nh=nh, w_gate=w_gate[l],
                  p_a=p_a_b[l], p_b=p_b_b[l], p_m=p_m_b[l], w_out=w_out_b[l], w_down=w_down_b[l],
                  g_post_mix=g_post_mix[l], g_pre_ffn=g_pre_ffn[l], g_post_ffn=g_post_ffn[l],
                  g_next=g_pre_mix[(l + 1) % depth])
        s5_args = (ab_re[l].reshape(1, gp), ab_im[l].reshape(1, gp), wbu, wc_re, wc_im,
                   d_skip[l].reshape(1, wb), w_glu_b[l])
        emit_h = l + 1 < depth

        def prompt_branches(q, kb, vb, lf, u, qm):
            c = cumsum_lanes(lf.reshape(Bp, L, nh).transpose(0, 2, 1))
            o_a = fox_prompt(q.reshape(Bp, L, wa), kb.reshape(Bp, L, wa), vb.reshape(Bp, L, wa),
                             c.reshape(Bp, nh, L, 1), c.reshape(Bp, nh, 1, L), nh).reshape(Bp * L, wa)
            u_tb = u.reshape(Bp, L, wb).transpose(1, 0, 2).reshape(L * Bp, wb)
            o_b, hre, him = s5_branch(u_tb, zeros_p, zeros_p, *s5_args, nb=Bp)
            o_b = o_b.reshape(L, Bp, wb).transpose(1, 0, 2).reshape(Bp * L, wb)
            o_m = mem_attn(qm.reshape(Bp, L, wm), mkb[l].reshape(Bp, n_mem, wm),
                           mvb[l].reshape(Bp, n_mem, wm)).reshape(Bp * L, wm)
            return o_a, o_b, o_m, (hre, him)

        def prompt_ffn(h2):
            act, sa, sb = ffn_up(h2, w_up_b[l], conv_w[l], conv_b[l].reshape(1, F2), Bp)
            return act, jnp.concatenate([sa, sb], axis=-1)

        xp, hp, k, v, lf, cv, (hre, him) = _layer_tokens(xp, hp, wl, prompt_branches, prompt_ffn, emit_h)
        for i, a in zip((0, 1, 2, 3, 4, 5, 6, 7), (
                k.reshape(Bp, L, nh, HEAD_DIM), v.reshape(Bp, L, nh, HEAD_DIM), lf.reshape(Bp, L, nh),
                mk[l].reshape(Bp, n_mem, nhm, HEAD_DIM), mv[l].reshape(Bp, n_mem, nhm, HEAD_DIM),
                hre.reshape(Bp, G, P), him.reshape(Bp, G, P), cv)):
            outs[i].append(a)

        def sample_branches(q, kb, vb, lf, u, qm):
            R = T * HEADS_PAD
            q4 = q.reshape(Bd, T, 1, nh, HEAD_DIM)
            sel = (jnp.arange(HEADS_PAD)[:, None] == head_ids[None, :]).astype(BF16)
            qblk = (q4 * sel[None, None, :, :, None]).reshape(Bd, R, wa)
            lf3 = lf.reshape(Bd, T, nh)
            lx = jnp.pad(lf3.transpose(0, 2, 1), ((0, 0), (0, HEADS_PAD - nh), (0, LANES - T)))
            lx = jnp.broadcast_to(lx[:, None], (Bd, T, HEADS_PAD, LANES)).reshape(Bd, R, LANES)
            padrows = lambda a: jnp.pad(a.reshape(Bd, T, wa), ((0, 0), (0, page - T), (0, 0)))
            o = fox_decode(l, page_table, qblk, lx, padrows(kb), padrows(vb), cache_k2, cache_v2, cache_lf_t, T)
            o5 = o.reshape(Bd, T, HEADS_PAD, nh, HEAD_DIM)
            o_a = o5[:, :, head_ids, head_ids, :].reshape(Bd * T, wa).astype(BF16)
            u_tb = u.reshape(Bd, T, wb).transpose(1, 0, 2).reshape(T * Bd, wb)
            o_b, hre, him = s5_branch(u_tb, state_ssm_re[l].reshape(Bd, gp), state_ssm_im[l].reshape(Bd, gp),
                                      *s5_args, nb=Bd)
            o_b = o_b.reshape(T, Bd, wb).transpose(1, 0, 2).reshape(Bd * T, wb)
            o_m = mem_attn(qm.reshape(Bd, T, wm), cache_mk_b[l], cache_mv_b[l]).reshape(Bd * T, wm)
            return o_a, o_b, o_m, (hre, him)

        def sample_ffn(h2):
            h_tb = h2.reshape(Bd, T, D).transpose(1, 0, 2).reshape(T * Bd, D)
            st_tb = state_conv[l].transpose(1, 0, 2).reshape(2 * Bd, F2)
            act, sa, sb = ffn_up_step(h_tb, w_up_b[l], conv_w[l], conv_b[l].reshape(1, F2), st_tb, Bd)
            act = act.reshape(T, Bd, F2 // 2).transpose(1, 0, 2).reshape(Bd * T, F2 // 2)
            cv = jnp.concatenate([sa, sb], axis=-1).reshape(2, Bd, F2).transpose(1, 0, 2)
            return act, cv

        xs, hs, k, v, lf, cv, (hre, him) = _layer_tokens(xs, hs, wl, sample_branches, sample_ffn, emit_h)
        for i, a in zip((8, 9, 10, 11, 12, 13), (
                k.reshape(Bd, T, nh, HEAD_DIM), v.reshape(Bd, T, nh, HEAD_DIM), lf.reshape(Bd, T, nh),
                hre.reshape(Bd, G, P), him.reshape(Bd, G, P), cv)):
            outs[i].append(a)

    return (xp.reshape(Bp, L, D), xs.reshape(Bd, T, D)) + tuple(jnp.stack(o) for o in outs)
```

```python
import functools
import math

import jax
import jax.numpy as jnp
from jax import lax
from jax.experimental import pallas as pl
from jax.experimental.pallas import tpu as pltpu

F32 = jnp.float32
BF16 = jnp.bfloat16

RMS_EPS = 1e-6
HEAD_DIM = 128
GROUP_CH = 16
LANES = 128
SUBLANES = 8
MXU_DIM = 256
VMEM_LIMIT = 56 * 1024 * 1024
S5_GROUPS_PER_CHUNK = MXU_DIM // GROUP_CH
HEADS_PAD = SUBLANES
DECODE_PAGES_PER_STEP = 16


def _tile(n, pref):
    return pref if n % pref == 0 else n


def _params(sem, vmem=VMEM_LIMIT):
    return pltpu.CompilerParams(dimension_semantics=sem, vmem_limit_bytes=vmem)


def _resident(shape, index_map):
    return pl.BlockSpec(shape, index_map, pipeline_mode=pl.Buffered(1))


def _dot(a, b):
    return jnp.dot(a, b, preferred_element_type=F32)


def _dot_nt(a, b):
    return lax.dot_general(a, b, (((1,), (1,)), ((), ())), preferred_element_type=F32)


def _rms(x, g):
    return x * lax.rsqrt(jnp.mean(x * x, axis=-1, keepdims=True) + RMS_EPS) * g


def _sigmoid(x):
    return 1.0 / (1.0 + jnp.exp(-x))


def _log_sigmoid(x):
    return jnp.minimum(x, 0.0) - jnp.log1p(jnp.exp(-jnp.abs(x)))


def _gelu_tanh(x):
    c = math.sqrt(2.0 / math.pi)
    return 0.5 * x * (1.0 + jnp.tanh(c * (x + 0.044715 * (x * x * x))))


def _rmsnorm_kernel(x_ref, g_ref, o_ref):
    o_ref[...] = _rms(x_ref[...], g_ref[...]).astype(o_ref.dtype)


def rmsnorm_bf16(x, g):
    M, D = x.shape
    bm = _tile(M, 512)
    return pl.pallas_call(
        _rmsnorm_kernel, grid=(M // bm,),
        in_specs=[pl.BlockSpec((bm, D), lambda i: (i, 0)), pl.BlockSpec((1, D), lambda i: (0, 0))],
        out_specs=pl.BlockSpec((bm, D), lambda i: (i, 0)),
        out_shape=jax.ShapeDtypeStruct((M, D), BF16),
        compiler_params=_params(("parallel",)), name="rmsnorm",
    )(x, g.reshape(1, D))


def _inproj_kernel(h_ref, w_ref, bf_ref, q_ref, k_ref, v_ref, kb_ref, vb_ref, u_ref, qm_ref, lf_ref,
                   *, wa, wb, wm, nh):
    h = h_ref[...]
    scale = HEAD_DIM ** -0.5
    n_seq, _, seq_rows, _ = q_ref.shape

    def proj(a, b):
        return _dot(h, w_ref[:, a:b])

    def put_heads(ref, val):
        for s in range(n_seq):
            for hd in range(nh):
                ref[s, hd] = val[s * seq_rows:(s + 1) * seq_rows, hd * HEAD_DIM:(hd + 1) * HEAD_DIM]

    put_heads(q_ref, (proj(0, wa) * scale).astype(BF16))
    k = proj(wa, 2 * wa)
    put_heads(k_ref, k)
    put_heads(kb_ref, k.astype(BF16))
    v = proj(2 * wa, 3 * wa)
    put_heads(v_ref, v)
    put_heads(vb_ref, v.astype(BF16))
    o = 3 * wa
    u_ref[...] = proj(o, o + wb)
    o += wb
    qm_ref[...] = (proj(o, o + wm) * scale).astype(BF16)
    o += wm
    fl = proj(o, o + LANES) + bf_ref[...]
    lf_ref[...] = _log_sigmoid(fl)[:, :nh]


def inproj(h, w, bf, wa, wb, wm, nh, n_seq):
    M, D = h.shape
    NW = w.shape[1]
    L = M // n_seq
    bm = _tile(L, 512) if L >= SUBLANES else M
    tiles_per_seq = max(L // bm, 1)
    seqs_per_tile = max(bm // L, 1)
    rows_per_seq = min(bm, L)
    row = lambda n: pl.BlockSpec((bm, n), lambda i: (i, 0))
    heads = pl.BlockSpec((seqs_per_tile, nh, rows_per_seq, HEAD_DIM),
                         lambda i: (i // tiles_per_seq, 0, i % tiles_per_seq, 0))
    hshape = lambda dt: jax.ShapeDtypeStruct((n_seq, nh, L, HEAD_DIM), dt)
    return pl.pallas_call(
        functools.partial(_inproj_kernel, wa=wa, wb=wb, wm=wm, nh=nh), grid=(M // bm,),
        in_specs=[row(D), _resident((D, NW), lambda i: (0, 0)), pl.BlockSpec((1, LANES), lambda i: (0, 0))],
        out_specs=[heads] * 5 + [row(wb), row(wm), row(nh)],
        out_shape=[hshape(BF16), hshape(F32), hshape(F32), hshape(BF16), hshape(BF16),
                   jax.ShapeDtypeStruct((M, wb), F32), jax.ShapeDtypeStruct((M, wm), BF16),
                   jax.ShapeDtypeStruct((M, nh), F32)],
        compiler_params=_params(("parallel",)), name="inproj",
    )(h, w, bf)


def _cumsum_kernel(x_ref, o_ref):
    x = x_ref[0]
    L = x.shape[-1]
    lane = lax.broadcasted_iota(jnp.int32, x.shape, 1)
    s = 1
    while s < L:
        x = x + jnp.where(lane >= s, pltpu.roll(x, s, axis=1), 0.0)
        s *= 2
    o_ref[0] = x


def cumsum_lanes(x):
    B, H, L = x.shape
    return pl.pallas_call(
        _cumsum_kernel, grid=(B,),
        in_specs=[pl.BlockSpec((1, H, L), lambda b: (b, 0, 0))],
        out_specs=pl.BlockSpec((1, H, L), lambda b: (b, 0, 0)),
        out_shape=jax.ShapeDtypeStruct((B, H, L), F32),
        compiler_params=_params(("parallel",)), name="cumsum",
    )(x)


def _fox_kernel(q_ref, k_ref, v_ref, cq_ref, ck_ref, o_ref, m_sc, l_sc, acc_sc):
    qi = pl.program_id(1)
    ki = pl.program_id(2)
    nh = q_ref.shape[0]

    @pl.when(ki == 0)
    def _():
        m_sc[...] = jnp.full(m_sc.shape, -jnp.inf, F32)
        l_sc[...] = jnp.zeros(l_sc.shape, F32)
        acc_sc[...] = jnp.zeros(acc_sc.shape, F32)

    def update(masked):
        for h in range(nh):
            s = _dot_nt(q_ref[h], k_ref[h]) + cq_ref[h] - ck_ref[h]
            if masked:
                row = lax.broadcasted_iota(jnp.int32, s.shape, 0)
                col = lax.broadcasted_iota(jnp.int32, s.shape, 1)
                s = jnp.where(col <= row, s, -jnp.inf)
            m_prev = m_sc[h]
            m_new = jnp.maximum(m_prev, jnp.max(s, axis=-1, keepdims=True))
            alpha = jnp.exp(m_prev - m_new)
            p = jnp.exp(s - m_new)
            l_sc[h] = alpha * l_sc[h] + jnp.sum(p, axis=-1, keepdims=True)
            acc_sc[h] = alpha * acc_sc[h] + _dot(p.astype(BF16), v_ref[h])
            m_sc[h] = m_new

    @pl.when(ki < qi)
    def _():
        update(False)

    @pl.when(ki == qi)
    def _():
        update(True)
        for h in range(nh):
            o_ref[:, h * HEAD_DIM:(h + 1) * HEAD_DIM] = (acc_sc[h] / l_sc[h]).astype(o_ref.dtype)


def fox_prompt(q, k, v, cq, ck):
    B, nh, L, _ = q.shape
    bq = _tile(L, 512)
    nq = L // bq
    qspec = pl.BlockSpec((None, nh, bq, HEAD_DIM), lambda b, qi, ki: (b, 0, qi, 0))
    kspec = pl.BlockSpec((None, nh, bq, HEAD_DIM), lambda b, qi, ki: (b, 0, jnp.minimum(ki, qi), 0))
    return pl.pallas_call(
        _fox_kernel, grid=(B, nq, nq),
        in_specs=[qspec, kspec, kspec,
                  pl.BlockSpec((None, nh, bq, 1), lambda b, qi, ki: (b, 0, qi, 0)),
                  pl.BlockSpec((None, nh, 1, bq), lambda b, qi, ki: (b, 0, 0, jnp.minimum(ki, qi)))],
        out_specs=pl.BlockSpec((None, bq, nh * HEAD_DIM), lambda b, qi, ki: (b, qi, 0)),
        out_shape=jax.ShapeDtypeStruct((B, L, nh * HEAD_DIM), BF16),
        scratch_shapes=[pltpu.VMEM((nh, bq, 1), F32), pltpu.VMEM((nh, bq, 1), F32),
                        pltpu.VMEM((nh, bq, HEAD_DIM), F32)],
        compiler_params=_params(("parallel", "parallel", "arbitrary")), name="fox_prompt",
    )(q, k, v, cq, ck)


def _fox_decode_kernel(pt_ref, q_ref, lx_ref, kn_ref, vn_ref, *refs, pps, n_new):
    kp = refs[:pps]
    vp = refs[pps:2 * pps]
    lp = refs[2 * pps:3 * pps]
    o_ref, m_sc, l_sc, acc_sc, suf_sc = refs[3 * pps:]
    del pt_ref
    step = pl.program_id(1)
    nh, tp, _ = q_ref.shape
    page = kn_ref.shape[1]

    @pl.when(step == 0)
    def _():
        m_sc[...] = jnp.full(m_sc.shape, -jnp.inf, F32)
        l_sc[...] = jnp.zeros(l_sc.shape, F32)
        acc_sc[...] = jnp.zeros(acc_sc.shape, F32)
        suf_sc[...] = jnp.zeros(suf_sc.shape, F32)

    ck = lx_ref[...]
    col = lax.broadcasted_iota(jnp.int32, ck.shape, 1)
    row_t = lax.broadcasted_iota(jnp.int32, ck.shape, 0) % tp
    s = 1
    while s < n_new:
        ck = ck + jnp.where(col >= s, pltpu.roll(ck, s, axis=1), 0.0)
        s *= 2
    cq = jnp.sum(jnp.where(col == row_t, ck, 0.0), axis=-1, keepdims=True)

    def update(sc, v_tiles):
        m_prev = m_sc[...]
        m_new = jnp.maximum(m_prev, jnp.max(sc, axis=-1, keepdims=True))
        alpha = jnp.exp(m_prev - m_new)
        p = jnp.exp(sc - m_new)
        l_sc[...] = alpha * l_sc[...] + jnp.sum(p, axis=-1, keepdims=True)
        pv = []
        for h in range(nh):
            ph = p[h * tp:(h + 1) * tp].astype(BF16)
            acc = _dot(ph[:, :page], v_tiles[0](h))
            for i in range(1, len(v_tiles)):
                acc += _dot(ph[:, i * page:(i + 1) * page], v_tiles[i](h))
            pv.append(acc)
        acc_sc[...] = alpha * acc_sc[...] + jnp.concatenate(pv, axis=0)
        m_sc[...] = m_new

    lane = lax.broadcasted_iota(jnp.int32, (HEADS_PAD, LANES), 1)
    after = suf_sc[...]
    biases = []
    for i in range(pps):
        lf = lp[i][...]
        inc = lf
        s = 1
        while s < LANES:
            inc = inc + jnp.where(lane < LANES - s, pltpu.roll(inc, LANES - s, axis=1), 0.0)
            s *= 2
        biases.append(inc - lf + after)
        after = after + jnp.broadcast_to(inc[:, 0:1], after.shape)
    suf_sc[...] = after
    bias = jnp.concatenate(biases, axis=1)

    rows = []
    for h in range(nh):
        qh = q_ref[h]
        sh = jnp.concatenate([_dot_nt(qh, kp[i][h].astype(BF16)) for i in range(pps)], axis=1)
        rows.append(sh + bias[h:h + 1])
    update(jnp.concatenate(rows, axis=0) + cq,
           [lambda h, i=i: vp[i][h].astype(BF16) for i in range(pps)])

    @pl.when(step == pl.num_programs(1) - 1)
    def _():
        sc = jnp.concatenate([_dot_nt(q_ref[h], kn_ref[h]) for h in range(nh)], axis=0) + cq - ck
        sc = jnp.where((col <= row_t) & (col < n_new), sc, -jnp.inf)
        update(sc, [lambda h: vn_ref[h]])
        o_ref[...] = acc_sc[...] / l_sc[...]


def fox_decode(layer, page_table, q, lx, kn, vn, cache_k, cache_v, cache_lf, n_new):
    Bd, nh, tp, _ = q.shape
    n_pages = page_table.shape[1]
    page = cache_k.shape[3]
    pps = DECODE_PAGES_PER_STEP if n_pages % DECODE_PAGES_PER_STEP == 0 else 1
    steps = n_pages // pps
    R = nh * tp

    def page_spec(i, shape):
        zeros = (0,) * len(shape)
        return pl.BlockSpec((None, None) + shape,
                            lambda b, s, pt: (layer, pt[b, n_pages - 1 - (s * pps + i)]) + zeros)

    def per_b(shape):
        zeros = (0,) * len(shape)
        return pl.BlockSpec((None,) + shape, lambda b, s, pt: (b,) + zeros)

    in_specs = ([per_b((nh, tp, HEAD_DIM)), per_b((R, LANES)), per_b((nh, page, HEAD_DIM)),
                 per_b((nh, page, HEAD_DIM))]
                + [page_spec(i, (nh, page, HEAD_DIM)) for i in range(pps)]
                + [page_spec(i, (nh, page, HEAD_DIM)) for i in range(pps)]
                + [page_spec(i, (HEADS_PAD, page)) for i in range(pps)])
    grid_spec = pltpu.PrefetchScalarGridSpec(
        num_scalar_prefetch=1, grid=(Bd, steps), in_specs=in_specs,
        out_specs=per_b((R, HEAD_DIM)),
        scratch_shapes=[pltpu.VMEM((R, 1), F32), pltpu.VMEM((R, 1), F32), pltpu.VMEM((R, HEAD_DIM), F32),
                        pltpu.VMEM((HEADS_PAD, LANES), F32)])
    return pl.pallas_call(
        functools.partial(_fox_decode_kernel, pps=pps, n_new=n_new), grid_spec=grid_spec,
        out_shape=jax.ShapeDtypeStruct((Bd, R, HEAD_DIM), F32),
        compiler_params=_params(("parallel", "arbitrary")), name="fox_decode",
    )(page_table, q, lx, kn, vn, *([cache_k] * pps), *([cache_v] * pps), *([cache_lf] * pps))


def _s5_disc_kernel(are_ref, aim_ref, ldt_ref, bre_ref, bim_ref, abre_ref, abim_ref, bbre_ref, bbim_ref):
    a_re = are_ref[0]
    a_im = aim_ref[0]
    dt = jnp.exp(ldt_ref[0])
    mag = jnp.exp(a_re * dt)
    ab_re = mag * jnp.cos(a_im * dt)
    ab_im = mag * jnp.sin(a_im * dt)
    den = a_re * a_re + a_im * a_im
    n_re = ab_re - 1.0
    q_re = (n_re * a_re + ab_im * a_im) / den
    q_im = (ab_im * a_re - n_re * a_im) / den
    b_re = bre_ref[0]
    b_im = bim_ref[0]
    abre_ref[0] = ab_re
    abim_ref[0] = ab_im
    bbre_ref[0] = q_re * b_re - q_im * b_im
    bbim_ref[0] = q_re * b_im + q_im * b_re


def s5_discretize(a_re, a_im, log_dt, b_re_t, b_im_t):
    Dp, G, P = a_re.shape
    C = b_re_t.shape[2]
    gp = pl.BlockSpec((1, G, 1, P), lambda l: (l, 0, 0, 0))
    gcp = pl.BlockSpec((1, G, C, P), lambda l: (l, 0, 0, 0))
    return pl.pallas_call(
        _s5_disc_kernel, grid=(Dp,),
        in_specs=[gp, gp, pl.BlockSpec((1, G, 1, 1), lambda l: (l, 0, 0, 0)), gcp, gcp],
        out_specs=[gp, gp, gcp, gcp],
        out_shape=[jax.ShapeDtypeStruct((Dp, G, 1, P), F32)] * 2 + [jax.ShapeDtypeStruct((Dp, G, C, P), F32)] * 2,
        compiler_params=_params(("parallel",)), name="s5_discretize",
    )(a_re.reshape(Dp, G, 1, P), a_im.reshape(Dp, G, 1, P), log_dt.reshape(Dp, G, 1, 1), b_re_t, b_im_t)


def _s5_kernel(u_ref, h0re_ref, h0im_ref, are_ref, aim_ref, wbu_ref, wcre_ref, wcim_ref, d_ref, wglu_ref,
               o_ref, hre_ref, him_ref, bre_sc, bim_sc, *, nb, lane_chunk):
    n = pl.program_id(0)
    rows, wb = u_ref.shape
    tc = rows // nb
    gp = bre_sc.shape[1]
    kc = wbu_ref.shape[1]
    sc = wbu_ref.shape[2] // 2
    n_chunks = wbu_ref.shape[0]

    @pl.when(n == 0)
    def _():
        hre_ref[...] = h0re_ref[...]
        him_ref[...] = h0im_ref[...]

    u = u_ref[...]
    ub = u.astype(BF16)
    for c in range(n_chunks):
        bu = _dot(ub[:, c * kc:(c + 1) * kc], wbu_ref[c])
        bre_sc[:, c * sc:(c + 1) * sc] = bu[:, :sc]
        bim_sc[:, c * sc:(c + 1) * sc] = bu[:, sc:]

    spt = SUBLANES // nb
    for c in range(gp // lane_chunk):
        cs = slice(c * lane_chunk, (c + 1) * lane_chunk)
        ar = jnp.broadcast_to(are_ref[:, cs], (SUBLANES, lane_chunk))
        ai = jnp.broadcast_to(aim_ref[:, cs], (SUBLANES, lane_chunk))
        band = lax.broadcasted_iota(jnp.int32, (SUBLANES, lane_chunk), 0) // nb

        def step(t, carry, cs=cs, ar=ar, ai=ai, band=band):
            hr, hi = carry
            r0 = pl.multiple_of(t * SUBLANES, SUBLANES)
            xr = bre_sc[pl.ds(r0, SUBLANES), cs]
            xi = bim_sc[pl.ds(r0, SUBLANES), cs]
            out_r = out_i = None
            for k in range(spt):
                hr, hi = ar * hr - ai * hi + xr, ar * hi + ai * hr + xi
                out_r = hr if k == 0 else jnp.where(band == k, hr, out_r)
                out_i = hi if k == 0 else jnp.where(band == k, hi, out_i)
                if spt > 1:
                    hr = pltpu.roll(hr, nb, axis=0)
                    hi = pltpu.roll(hi, nb, axis=0)
            bre_sc[pl.ds(r0, SUBLANES), cs] = out_r
            bim_sc[pl.ds(r0, SUBLANES), cs] = out_i
            return hr, hi

        init = lambda ref: jnp.concatenate([ref[:, cs]] * spt, axis=0)
        hr, hi = lax.fori_loop(0, rows // SUBLANES, step, (init(hre_ref), init(him_ref)))
        hre_ref[:, cs] = hr[:nb]
        him_ref[:, cs] = hi[:nb]

    ys = []
    for c in range(n_chunks):
        hr = bre_sc[:, c * sc:(c + 1) * sc].astype(BF16)
        hi = bim_sc[:, c * sc:(c + 1) * sc].astype(BF16)
        ys.append(_dot(hr, wcre_ref[c]) - _dot(hi, wcim_ref[c]))
    y = jnp.concatenate(ys, axis=1) + d_ref[...] * u
    z = _gelu_tanh(y)
    o_ref[...] = (z * _sigmoid(_dot(z.astype(BF16), wglu_ref[...]))).astype(o_ref.dtype)


def s5_branch(u_tb, h0_re, h0_im, ab_re, ab_im, wbu, wc_re, wc_im, d_skip, w_glu, nb):
    rows_all, wb = u_tb.shape
    L = rows_all // nb
    gp = h0_re.shape[1]
    tc = _tile(L, 128)
    rows = tc * nb
    lane_chunk = _tile(gp, 768)
    full = lambda a: _resident(a.shape, lambda n: (0,) * a.ndim)
    st = pl.BlockSpec((nb, gp), lambda n: (0, 0))
    return pl.pallas_call(
        functools.partial(_s5_kernel, nb=nb, lane_chunk=lane_chunk), grid=(L // tc,),
        in_specs=[pl.BlockSpec((rows, wb), lambda n: (n, 0)), st, st, full(ab_re), full(ab_im),
                  full(wbu), full(wc_re), full(wc_im), full(d_skip), full(w_glu)],
        out_specs=[pl.BlockSpec((rows, wb), lambda n: (n, 0)), st, st],
        out_shape=[jax.ShapeDtypeStruct((rows_all, wb), BF16), jax.ShapeDtypeStruct((nb, gp), F32),
                   jax.ShapeDtypeStruct((nb, gp), F32)],
        scratch_shapes=[pltpu.VMEM((rows, gp), F32), pltpu.VMEM((rows, gp), F32)],
        compiler_params=_params(("arbitrary",)), name="s5_branch",
    )(u_tb, h0_re, h0_im, ab_re, ab_im, wbu, wc_re, wc_im, d_skip, w_glu)


def _mem_kv_kernel(x_ref, g_ref, w_ref, k_ref, v_ref, kb_ref, vb_ref):
    h = _rms(x_ref[...], g_ref[0]).astype(BF16)
    z = _dot(h, w_ref[0])
    wm = k_ref.shape[-1]
    k_ref[0] = z[:, :wm]
    v_ref[0] = z[:, wm:]
    kb_ref[0] = z[:, :wm].astype(BF16)
    vb_ref[0] = z[:, wm:].astype(BF16)


def mem_kv(mem, g_mem, w_mkv):
    R, D = mem.shape
    Dp, _, W2 = w_mkv.shape
    wm = W2 // 2
    bm = _tile(R, 512)
    o = pl.BlockSpec((1, bm, wm), lambda l, i: (l, i, 0))
    return pl.pallas_call(
        _mem_kv_kernel, grid=(Dp, R // bm),
        in_specs=[pl.BlockSpec((bm, D), lambda l, i: (i, 0)), pl.BlockSpec((1, 1, D), lambda l, i: (l, 0, 0)),
                  pl.BlockSpec((1, D, W2), lambda l, i: (l, 0, 0))],
        out_specs=[o, o, o, o],
        out_shape=[jax.ShapeDtypeStruct((Dp, R, wm), F32)] * 2 + [jax.ShapeDtypeStruct((Dp, R, wm), BF16)] * 2,
        compiler_params=_params(("parallel", "arbitrary")), name="mem_kv",
    )(mem, g_mem, w_mkv)


def _mem_attn_kernel(q_ref, k_ref, v_ref, o_ref):
    nh = q_ref.shape[-1] // HEAD_DIM
    for h in range(nh):
        hs = slice(h * HEAD_DIM, (h + 1) * HEAD_DIM)
        s = _dot_nt(q_ref[:, hs], k_ref[:, hs])
        p = jnp.exp(s - jnp.max(s, axis=-1, keepdims=True))
        p = p / jnp.sum(p, axis=-1, keepdims=True)
        o_ref[:, hs] = _dot(p.astype(BF16), v_ref[:, hs]).astype(o_ref.dtype)


def mem_attn(q, k, v):
    B, L, W = q.shape
    Nm = k.shape[1]
    bq = _tile(L, 512)
    qs = pl.BlockSpec((None, bq, W), lambda b, i: (b, i, 0))
    ks = pl.BlockSpec((None, Nm, W), lambda b, i: (b, 0, 0))
    return pl.pallas_call(
        _mem_attn_kernel, grid=(B, L // bq), in_specs=[qs, ks, ks], out_specs=qs,
        out_shape=jax.ShapeDtypeStruct((B, L, W), BF16),
        compiler_params=_params(("parallel", "parallel")), name="mem_attn",
    )(q, k, v)


def _merge_kernel(h_ref, oa_ref, ob_ref, om_ref, g0_ref, g1_ref, g2_ref, pa_ref, pb_ref, pm_ref, o_ref):
    h = h_ref[...]
    acc = _sigmoid(_dot(h, g0_ref[...])) * _dot(oa_ref[...], pa_ref[...])
    acc += _sigmoid(_dot(h, g1_ref[...])) * _dot(ob_ref[...], pb_ref[...])
    acc += _sigmoid(_dot(h, g2_ref[...])) * _dot(om_ref[...], pm_ref[...])
    o_ref[...] = acc.astype(o_ref.dtype)


def merge(h, oa, ob, om, w_gate, pa, pb, pm):
    M, D = h.shape
    bm = _tile(M, 1024)
    bn = _tile(D, 512)
    nj = D // bn
    rowspec = lambda a: pl.BlockSpec((bm, a.shape[1]), lambda j, i: (i, 0))
    colspec = lambda a: pl.BlockSpec((a.shape[0], bn), lambda j, i: (0, j))
    gate = lambda b: pl.BlockSpec((D, bn), lambda j, i: (0, b * nj + j))
    return pl.pallas_call(
        _merge_kernel, grid=(nj, M // bm),
        in_specs=[rowspec(h), rowspec(oa), rowspec(ob), rowspec(om), gate(0), gate(1), gate(2),
                  colspec(pa), colspec(pb), colspec(pm)],
        out_specs=pl.BlockSpec((bm, bn), lambda j, i: (i, j)),
        out_shape=jax.ShapeDtypeStruct((M, D), BF16),
        compiler_params=_params(("parallel", "arbitrary")), name="merge",
    )(h, oa, ob, om, w_gate, w_gate, w_gate, pa, pb, pm)


def _proj_res_kernel(a_ref, w_ref, x_ref, gpost_ref, gnext_ref, xo_ref, *h_ref):
    f = _dot(a_ref[...], w_ref[...])
    x = x_ref[...] + _rms(f, gpost_ref[...])
    xo_ref[...] = x
    if h_ref:
        h_ref[0][...] = _rms(x, gnext_ref[...]).astype(BF16)


def proj_residual(a, w, x, g_post, g_next, emit_h=True):
    M, K = a.shape
    D = w.shape[1]
    bm = _tile(M, 256)
    row = lambda n: pl.BlockSpec((bm, n), lambda i: (i, 0))
    g = pl.BlockSpec((1, D), lambda i: (0, 0))
    out_specs = [row(D)] + ([row(D)] if emit_h else [])
    out_shape = [jax.ShapeDtypeStruct((M, D), F32)] + ([jax.ShapeDtypeStruct((M, D), BF16)] if emit_h else [])
    res = pl.pallas_call(
        _proj_res_kernel, grid=(M // bm,),
        in_specs=[row(K), _resident((K, D), lambda i: (0, 0)), row(D), g, g],
        out_specs=out_specs, out_shape=out_shape,
        compiler_params=_params(("parallel",)), name="proj_residual",
    )(a, w, x, g_post.reshape(1, D), g_next.reshape(1, D))
    return (res[0], res[1]) if emit_h else (res[0], None)


def _ffn_up_kernel(h_ref, wa_ref, wb_ref, cwa_ref, cwb_ref, cba_ref, cbb_ref, act_ref, sa_ref, sb_ref,
                   ca_sc, cb_sc, *, tiles_per_seq):
    i = pl.program_id(1)
    h = h_ref[...]
    bm = h.shape[0]

    @pl.when(i % tiles_per_seq == 0)
    def _():
        ca_sc[...] = jnp.zeros(ca_sc.shape, F32)
        cb_sc[...] = jnp.zeros(cb_sc.shape, F32)

    def conv(w_ref, c_sc, cw_ref, cb_ref, s_ref):
        u = _dot(h, w_ref[...])
        prev = c_sc[...]
        row = lax.broadcasted_iota(jnp.int32, u.shape, 0)
        u1 = jnp.where(row == 0, prev[SUBLANES - 1:SUBLANES], pltpu.roll(u, 1, axis=0))
        u2 = jnp.where(row == 0, prev[SUBLANES - 2:SUBLANES - 1],
                       jnp.where(row == 1, prev[SUBLANES - 1:SUBLANES], pltpu.roll(u, 2, axis=0)))
        c_sc[...] = u[bm - SUBLANES:]
        s_ref[...] = u[bm - 2:]
        return cb_ref[...] + cw_ref[2:3] * u + cw_ref[1:2] * u1 + cw_ref[0:1] * u2

    ya = conv(wa_ref, ca_sc, cwa_ref, cba_ref, sa_ref)
    yb = conv(wb_ref, cb_sc, cwb_ref, cbb_ref, sb_ref)
    act_ref[...] = (_gelu_tanh(ya) * yb).astype(act_ref.dtype)


def ffn_up(h, w_up, conv_w, conv_b, n_seq):
    M, D = h.shape
    F2 = w_up.shape[1]
    F = F2 // 2
    L = M // n_seq
    bm = _tile(L, 1024)
    bn = _tile(F, 512)
    nj = F // bn
    tps = L // bm
    a_col = lambda r: pl.BlockSpec((r, bn), lambda j, i: (0, j))
    b_col = lambda r: pl.BlockSpec((r, bn), lambda j, i: (0, nj + j))
    st = pl.BlockSpec((None, 2, bn), lambda j, i: (i // tps, 0, j))
    return pl.pallas_call(
        functools.partial(_ffn_up_kernel, tiles_per_seq=tps), grid=(nj, M // bm),
        in_specs=[pl.BlockSpec((bm, D), lambda j, i: (i, 0)), a_col(D), b_col(D), a_col(3), b_col(3),
                  a_col(1), b_col(1)],
        out_specs=[pl.BlockSpec((bm, bn), lambda j, i: (i, j)), st, st],
        out_shape=[jax.ShapeDtypeStruct((M, F), BF16), jax.ShapeDtypeStruct((n_seq, 2, F), F32),
                   jax.ShapeDtypeStruct((n_seq, 2, F), F32)],
        scratch_shapes=[pltpu.VMEM((SUBLANES, bn), F32), pltpu.VMEM((SUBLANES, bn), F32)],
        compiler_params=_params(("parallel", "arbitrary")), name="ffn_up",
    )(h, w_up, w_up, conv_w, conv_w, conv_b, conv_b)


def _ffn_up_step_kernel(h_ref, wa_ref, wb_ref, sta_ref, stb_ref, cwa_ref, cwb_ref, cba_ref, cbb_ref,
                        act_ref, sa_ref, sb_ref, *, n_seq):
    h = h_ref[...]
    T = h.shape[0] // n_seq

    def conv(w_ref, st_ref, cw_ref, cb_ref, s_ref):
        u = _dot(h, w_ref[...])
        ext = jnp.concatenate([st_ref[...], u], axis=0)
        s_ref[...] = ext[T * n_seq:]
        return (cb_ref[...] + cw_ref[2:3] * u + cw_ref[1:2] * ext[n_seq:(T + 1) * n_seq]
                + cw_ref[0:1] * ext[:T * n_seq])

    ya = conv(wa_ref, sta_ref, cwa_ref, cba_ref, sa_ref)
    yb = conv(wb_ref, stb_ref, cwb_ref, cbb_ref, sb_ref)
    act_ref[...] = (_gelu_tanh(ya) * yb).astype(act_ref.dtype)


def ffn_up_step(h_tb, w_up, conv_w, conv_b, state_tb, n_seq):
    M, D = h_tb.shape
    F2 = w_up.shape[1]
    F = F2 // 2
    bn = _tile(F, 512)
    nj = F // bn
    a_col = lambda r: pl.BlockSpec((r, bn), lambda j: (0, j))
    b_col = lambda r: pl.BlockSpec((r, bn), lambda j: (0, nj + j))
    return pl.pallas_call(
        functools.partial(_ffn_up_step_kernel, n_seq=n_seq), grid=(nj,),
        in_specs=[pl.BlockSpec((M, D), lambda j: (0, 0)), a_col(D), b_col(D), a_col(2 * n_seq), b_col(2 * n_seq),
                  a_col(3), b_col(3), a_col(1), b_col(1)],
        out_specs=[a_col(M), a_col(2 * n_seq), a_col(2 * n_seq)],
        out_shape=[jax.ShapeDtypeStruct((M, F), BF16), jax.ShapeDtypeStruct((2 * n_seq, F), F32),
                   jax.ShapeDtypeStruct((2 * n_seq, F), F32)],
        compiler_params=_params(("parallel",)), name="ffn_up_step",
    )(h_tb, w_up, w_up, state_tb, state_tb, conv_w, conv_w, conv_b, conv_b)


def _s5_weights(bb_re_t, bb_im_t, c_re, c_im):
    G, C, P = bb_re_t.shape
    gc = S5_GROUPS_PER_CHUNK if G % S5_GROUPS_PER_CHUNK == 0 else G
    n = G // gc
    eye = jnp.eye(gc, dtype=F32)

    def bu(bb):
        return jnp.einsum('ngcp,gh->ngchp', bb.reshape(n, gc, C, P), eye).reshape(n, gc * C, gc * P)

    def cy(c):
        return jnp.einsum('ngcp,gh->ngphc', c.reshape(n, gc, C, P), eye).reshape(n, gc * P, gc * C)

    wbu = jnp.concatenate([bu(bb_re_t), bu(bb_im_t)], axis=-1).astype(BF16)
    return wbu, cy(c_re).astype(BF16), cy(c_im).astype(BF16)


def _layer_tokens(x, h, wl, n_seq, branch_fn, ffn_fn, emit_h):
    q, k, v, kb, vb, u, qm, lf = inproj(h, wl['w_proj'], wl['b_f'], wl['wa'], wl['wb'], wl['wm'], wl['nh'],
                                        n_seq)
    o_a, o_b, o_m, extra = branch_fn(q, kb, vb, lf, u, qm)
    merged = merge(h, o_a, o_b, o_m, wl['w_gate'], wl['p_a'], wl['p_b'], wl['p_m'])
    x, h2 = proj_residual(merged, wl['w_out'], x, wl['g_post_mix'], wl['g_pre_ffn'])
    act, conv_new = ffn_fn(h2)
    x, h_next = proj_residual(act, wl['w_down'], x, wl['g_post_ffn'], wl['g_next'], emit_h=emit_h)
    return x, h_next, k, v, lf, conv_new, extra


def kernel(x_prompt, x_sample, cache_k, cache_v, cache_logf, cache_mem_k, cache_mem_v, state_ssm_re, state_ssm_im, state_conv, page_table, mem_prompt, w_in, b_f, a_re, a_im, log_dt, b_re, b_im, c_re, c_im, d_skip, w_glu, g_mem, w_mkv, p_a, p_b, p_m, w_out, g_pre_mix, g_post_mix, g_pre_ffn, g_post_ffn, w_up, conv_w, conv_b, w_down):
    Bp, L, D = x_prompt.shape
    Bd, T, _ = x_sample.shape
    depth = w_in.shape[0]
    wa = p_a.shape[1]
    wb = p_b.shape[1]
    wm = p_m.shape[1]
    nh = b_f.shape[1]
    nhm = wm // HEAD_DIM
    G, P = a_re.shape[1:]
    C = b_re.shape[3]
    gp = G * P
    n_mem = mem_prompt.shape[1]
    n_pool, page = cache_k.shape[1:3]
    F2 = w_up.shape[2]

    o_fl = 3 * wa
    o_u = o_fl + nh
    o_qm = o_u + wb
    o_g = o_qm + wm
    w_fl = jnp.pad(w_in[:, :, o_fl:o_u], ((0, 0), (0, 0), (0, LANES - nh)))
    w_proj = jnp.concatenate([w_in[:, :, :o_fl], w_in[:, :, o_u:o_g], w_fl], axis=-1).astype(BF16)
    w_gate = w_in[:, :, o_g:].astype(BF16)
    b_f_pad = jnp.pad(b_f, ((0, 0), (0, LANES - nh))).reshape(depth, 1, LANES)
    ab_re, ab_im, bb_re_t, bb_im_t = s5_discretize(a_re, a_im, log_dt, b_re.transpose(0, 1, 3, 2),
                                                   b_im.transpose(0, 1, 3, 2))
    bf = lambda a: a.astype(BF16)
    w_mkv_b, p_a_b, p_b_b, p_m_b, w_out_b, w_up_b, w_down_b, w_glu_b = map(
        bf, (w_mkv, p_a, p_b, p_m, w_out, w_up, w_down, w_glu))

    mk, mv, mkb, mvb = mem_kv(mem_prompt.reshape(Bp * n_mem, D), g_mem.reshape(depth, 1, D), w_mkv_b)

    cache_k2 = cache_k.transpose(0, 1, 3, 2, 4)
    cache_v2 = cache_v.transpose(0, 1, 3, 2, 4)
    cache_lf_t = jnp.pad(cache_logf.transpose(0, 1, 3, 2), ((0, 0), (0, 0), (0, HEADS_PAD - nh), (0, 0)))
    cache_mk_b = bf(cache_mem_k).reshape(depth, Bd, n_mem, wm)
    cache_mv_b = bf(cache_mem_v).reshape(depth, Bd, n_mem, wm)

    xp = x_prompt.reshape(Bp * L, D)
    xs = x_sample.reshape(Bd * T, D)
    hp = rmsnorm_bf16(xp, g_pre_mix[0])
    hs = rmsnorm_bf16(xs, g_pre_mix[0])
    zeros_p = jnp.zeros((Bp, gp), F32)
    head_ids = jnp.arange(nh)

    outs = [[] for _ in range(14)]
    for l in range(depth):
        wbu, wc_re, wc_im = _s5_weights(bb_re_t[l], bb_im_t[l], c_re[l], c_im[l])
        wl = dict(w_proj=w_proj[l], b_f=b_f_pad[l], wa=wa, wb=wb, wm=wm, nh=nh, w_gate=w_gate[l],
                  p_a=p_a_b[l], p_b=p_b_b[l], p_m=p_m_b[l], w_out=w_out_b[l], w_down=w_down_b[l],
                  g_post_mix=g_post_mix[l], g_pre_ffn=g_pre_ffn[l], g_post_ffn=g_post_ffn[l],
                  g_next=g_pre_mix[(l + 1) % depth])
        s5_args = (ab_re[l].reshape(1, gp), ab_im[l].reshape(1, gp), wbu, wc_re, wc_im,
                   d_skip[l].reshape(1, wb), w_glu_b[l])
        emit_h = l + 1 < depth

        def prompt_branches(q, kb, vb, lf, u, qm):
            c = cumsum_lanes(lf.reshape(Bp, L, nh).transpose(0, 2, 1))
            o_a = fox_prompt(q, kb, vb, c.reshape(Bp, nh, L, 1), c.reshape(Bp, nh, 1, L)).reshape(Bp * L, wa)
            u_tb = u.reshape(Bp, L, wb).transpose(1, 0, 2).reshape(L * Bp, wb)
            o_b, hre, him = s5_branch(u_tb, zeros_p, zeros_p, *s5_args, nb=Bp)
            o_b = o_b.reshape(L, Bp, wb).transpose(1, 0, 2).reshape(Bp * L, wb)
            o_m = mem_attn(qm.reshape(Bp, L, wm), mkb[l].reshape(Bp, n_mem, wm),
                           mvb[l].reshape(Bp, n_mem, wm)).reshape(Bp * L, wm)
            return o_a, o_b, o_m, (hre, him)

        def prompt_ffn(h2):
            act, sa, sb = ffn_up(h2, w_up_b[l], conv_w[l], conv_b[l].reshape(1, F2), Bp)
            return act, jnp.concatenate([sa, sb], axis=-1)

        xp, hp, k, v, lf, cv, (hre, him) = _layer_tokens(xp, hp, wl, Bp, prompt_branches, prompt_ffn, emit_h)
        for i, a in zip((0, 1, 2, 3, 4, 5, 6, 7), (
                k, v, lf.reshape(Bp, L, nh),
                mk[l].reshape(Bp, n_mem, nhm, HEAD_DIM), mv[l].reshape(Bp, n_mem, nhm, HEAD_DIM),
                hre.reshape(Bp, G, P), him.reshape(Bp, G, P), cv)):
            outs[i].append(a)

        def sample_branches(q, kb, vb, lf, u, qm):
            tp = -(-T // SUBLANES) * SUBLANES
            padrows = lambda a, n: jnp.pad(a, ((0, 0), (0, 0), (0, n - T), (0, 0)))
            lx = jnp.pad(lf.reshape(Bd, T, nh).transpose(0, 2, 1), ((0, 0), (0, 0), (0, LANES - T)))
            lx = jnp.broadcast_to(lx[:, :, None], (Bd, nh, tp, LANES)).reshape(Bd, nh * tp, LANES)
            o = fox_decode(l, page_table, padrows(q, tp), lx, padrows(kb, page), padrows(vb, page),
                           cache_k2, cache_v2, cache_lf_t, T)
            o_a = o.reshape(Bd, nh, tp, HEAD_DIM)[:, :, :T].transpose(0, 2, 1, 3).reshape(Bd * T, wa).astype(BF16)
            u_tb = u.reshape(Bd, T, wb).transpose(1, 0, 2).reshape(T * Bd, wb)
            o_b, hre, him = s5_branch(u_tb, state_ssm_re[l].reshape(Bd, gp), state_ssm_im[l].reshape(Bd, gp),
                                      *s5_args, nb=Bd)
            o_b = o_b.reshape(T, Bd, wb).transpose(1, 0, 2).reshape(Bd * T, wb)
            o_m = mem_attn(qm.reshape(Bd, T, wm), cache_mk_b[l], cache_mv_b[l]).reshape(Bd * T, wm)
            return o_a, o_b, o_m, (hre, him)

        def sample_ffn(h2):
            h_tb = h2.reshape(Bd, T, D).transpose(1, 0, 2).reshape(T * Bd, D)
            st_tb = state_conv[l].transpose(1, 0, 2).reshape(2 * Bd, F2)
            act, sa, sb = ffn_up_step(h_tb, w_up_b[l], conv_w[l], conv_b[l].reshape(1, F2), st_tb, Bd)
            act = act.reshape(T, Bd, F2 // 2).transpose(1, 0, 2).reshape(Bd * T, F2 // 2)
            cv = jnp.concatenate([sa, sb], axis=-1).reshape(2, Bd, F2).transpose(1, 0, 2)
            return act, cv

        xs, hs, k, v, lf, cv, (hre, him) = _layer_tokens(xs, hs, wl, Bd, sample_branches, sample_ffn, emit_h)
        for i, a in zip((8, 9, 10, 11, 12, 13), (
                k, v, lf.reshape(Bd, T, nh),
                hre.reshape(Bd, G, P), him.reshape(Bd, G, P), cv)):
            outs[i].append(a)

    res = [jnp.stack(o) for o in outs]
    for i in (0, 1, 8, 9):
        res[i] = res[i].transpose(0, 1, 3, 2, 4)
    return (xp.reshape(Bp, L, D), xs.reshape(Bd, T, D)) + tuple(res)
```

```python
import functools
import math

import jax
import jax.numpy as jnp
from jax import lax
from jax.experimental import pallas as pl
from jax.experimental.pallas import tpu as pltpu

F32 = jnp.float32
BF16 = jnp.bfloat16

RMS_EPS = 1e-6
HEAD_DIM = 128
GROUP_CH = 16
LANES = 128
SUBLANES = 8
MXU_DIM = 256
VMEM_LIMIT = 56 * 1024 * 1024
S5_GROUPS_PER_CHUNK = MXU_DIM // GROUP_CH
HEADS_PAD = SUBLANES
GATE_ALIGN = 512
LOG2E = 1.4426950408889634
DECODE_PAGES_PER_STEP = 16


def _tile(n, pref):
    return pref if n % pref == 0 else n


def _params(sem, vmem=VMEM_LIMIT):
    return pltpu.CompilerParams(dimension_semantics=sem, vmem_limit_bytes=vmem)


def _resident(shape, index_map):
    return pl.BlockSpec(shape, index_map, pipeline_mode=pl.Buffered(1))


def _dot(a, b):
    return jnp.dot(a, b, preferred_element_type=F32)


def _dot_nt(a, b):
    return lax.dot_general(a, b, (((1,), (1,)), ((), ())), preferred_element_type=F32)


def _rms(x, g):
    return x * lax.rsqrt(jnp.mean(x * x, axis=-1, keepdims=True) + RMS_EPS) * g


def _sigmoid(x):
    return 1.0 / (1.0 + jnp.exp(-x))


def _log_sigmoid(x):
    return jnp.minimum(x, 0.0) - jnp.log1p(jnp.exp(-jnp.abs(x)))


def _gelu_tanh(x):
    c = math.sqrt(2.0 / math.pi)
    return 0.5 * x * (1.0 + jnp.tanh(c * (x + 0.044715 * (x * x * x))))


def _rmsnorm_kernel(x_ref, g_ref, o_ref):
    o_ref[...] = _rms(x_ref[...], g_ref[...]).astype(o_ref.dtype)


def rmsnorm_bf16(x, g):
    M, D = x.shape
    bm = _tile(M, 512)
    return pl.pallas_call(
        _rmsnorm_kernel, grid=(M // bm,),
        in_specs=[pl.BlockSpec((bm, D), lambda i: (i, 0)), pl.BlockSpec((1, D), lambda i: (0, 0))],
        out_specs=pl.BlockSpec((bm, D), lambda i: (i, 0)),
        out_shape=jax.ShapeDtypeStruct((M, D), BF16),
        compiler_params=_params(("parallel",)), name="rmsnorm",
    )(x, g.reshape(1, D))


def _inproj_kernel(h_ref, w_ref, bf_ref, q_ref, k_ref, v_ref, kb_ref, vb_ref, u_ref, qm_ref, lf_ref,
                   *, wa, wb, wm, nh, q_scale):
    h = h_ref[...]
    scale = HEAD_DIM ** -0.5
    n_seq, _, seq_rows, _ = q_ref.shape

    def proj(a, b):
        return _dot(h, w_ref[:, a:b])

    def put_heads(ref, val):
        for s in range(n_seq):
            for hd in range(nh):
                ref[s, hd] = val[s * seq_rows:(s + 1) * seq_rows, hd * HEAD_DIM:(hd + 1) * HEAD_DIM]

    put_heads(q_ref, (proj(0, wa) * q_scale).astype(BF16))
    k = proj(wa, 2 * wa)
    put_heads(k_ref, k)
    put_heads(kb_ref, k.astype(BF16))
    v = proj(2 * wa, 3 * wa)
    put_heads(v_ref, v)
    put_heads(vb_ref, v.astype(BF16))
    o = 3 * wa
    fl = proj(o, o + LANES) + bf_ref[...]
    lf_ref[...] = _log_sigmoid(fl)[:, :nh]
    o += LANES
    u_ref[...] = proj(o, o + wb)
    o += wb
    qm_ref[...] = (proj(o, o + wm) * scale).astype(BF16)


def inproj(h, w, layer, n_proj, bf, wa, wb, wm, nh, n_seq, q_scale):
    M, D = h.shape
    NW = n_proj
    L = M // n_seq
    bm = _tile(L, 512) if L >= SUBLANES else M
    tiles_per_seq = max(L // bm, 1)
    seqs_per_tile = max(bm // L, 1)
    rows_per_seq = min(bm, L)
    row = lambda n: pl.BlockSpec((bm, n), lambda i: (i, 0))
    heads = pl.BlockSpec((seqs_per_tile, nh, rows_per_seq, HEAD_DIM),
                         lambda i: (i // tiles_per_seq, 0, i % tiles_per_seq, 0))
    hshape = lambda dt: jax.ShapeDtypeStruct((n_seq, nh, L, HEAD_DIM), dt)
    return pl.pallas_call(
        functools.partial(_inproj_kernel, wa=wa, wb=wb, wm=wm, nh=nh, q_scale=q_scale), grid=(M // bm,),
        in_specs=[row(D), _resident((None, D, NW), lambda i: (layer, 0, 0)),
                  pl.BlockSpec((1, LANES), lambda i: (0, 0))],
        out_specs=[heads] * 5 + [row(wb), row(wm), row(nh)],
        out_shape=[hshape(BF16), hshape(F32), hshape(F32), hshape(BF16), hshape(BF16),
                   jax.ShapeDtypeStruct((M, wb), F32), jax.ShapeDtypeStruct((M, wm), BF16),
                   jax.ShapeDtypeStruct((M, nh), F32)],
        compiler_params=_params(("parallel",)), name="inproj",
    )(h, w, bf)


def _cumsum_kernel(x_ref, o_ref):
    x = x_ref[0]
    L = x.shape[-1]
    lane = lax.broadcasted_iota(jnp.int32, x.shape, 1)
    s = 1
    while s < L:
        x = x + jnp.where(lane >= s, pltpu.roll(x, s, axis=1), 0.0)
        s *= 2
    o_ref[0] = x


def cumsum_lanes(x):
    B, H, L = x.shape
    return pl.pallas_call(
        _cumsum_kernel, grid=(B,),
        in_specs=[pl.BlockSpec((1, H, L), lambda b: (b, 0, 0))],
        out_specs=pl.BlockSpec((1, H, L), lambda b: (b, 0, 0)),
        out_shape=jax.ShapeDtypeStruct((B, H, L), F32),
        compiler_params=_params(("parallel",)), name="cumsum",
    )(x)


def _fox_kernel(q_ref, qa_ref, k_ref, ka_ref, v_ref, o_ref, m_sc, l_sc, acc_sc):
    qi = pl.program_id(1)
    ki = pl.program_id(2)
    nh = q_ref.shape[0]

    @pl.when(ki == 0)
    def _():
        m_sc[...] = jnp.full(m_sc.shape, -jnp.inf, F32)
        l_sc[...] = jnp.zeros(l_sc.shape, F32)
        acc_sc[...] = jnp.zeros(acc_sc.shape, F32)

    def update(masked):
        heads = range(nh)
        s = [_dot_nt(jnp.concatenate([q_ref[h], qa_ref[h]], axis=1),
                     jnp.concatenate([k_ref[h], ka_ref[h]], axis=1)) for h in heads]
        if masked:
            row = lax.broadcasted_iota(jnp.int32, s[0].shape, 0)
            col = lax.broadcasted_iota(jnp.int32, s[0].shape, 1)
            s = [jnp.where(col <= row, sh, -jnp.inf) for sh in s]
        m_prev = [m_sc[h] for h in heads]
        m_new = [jnp.maximum(m_prev[h], jnp.max(s[h], axis=-1, keepdims=True)) for h in heads]
        alpha = [jnp.exp2(m_prev[h] - m_new[h]) for h in heads]
        p = [jnp.exp2(s[h] - m_new[h]) for h in heads]
        for h in heads:
            l_sc[h] = alpha[h] * l_sc[h] + jnp.sum(p[h], axis=-1, keepdims=True)
            m_sc[h] = m_new[h]
        pv = [_dot(p[h].astype(BF16), v_ref[h]) for h in heads]
        for h in heads:
            acc_sc[h] = alpha[h] * acc_sc[h] + pv[h]

    @pl.when(ki < qi)
    def _():
        update(False)

    @pl.when(ki == qi)
    def _():
        update(True)
        for h in range(nh):
            o_ref[:, h * HEAD_DIM:(h + 1) * HEAD_DIM] = (acc_sc[h] / l_sc[h]).astype(o_ref.dtype)


def fox_bias_features(c):
    def top_half(x):
        bits = lax.bitcast_convert_type(x, jnp.uint32) & jnp.uint32(0xFFFF0000)
        return lax.bitcast_convert_type(bits, F32)

    c2 = c * LOG2E
    hi_f = top_half(c2)
    mid_f = top_half(c2 - hi_f)
    hi, mid, lo = hi_f.astype(BF16), mid_f.astype(BF16), (c2 - hi_f - mid_f).astype(BF16)
    one = jnp.ones_like(hi)
    widen = lambda cols: jnp.pad(jnp.stack(cols, axis=-1), ((0, 0),) * 3 + ((0, HEAD_DIM - len(cols)),))
    return widen([hi, mid, lo, one, one, one]), widen([one, one, one, -hi, -mid, -lo])


def fox_prompt(q, qa, k, ka, v):
    B, nh, L, _ = q.shape
    bq = _tile(L, 512)
    nq = L // bq
    qspec = pl.BlockSpec((None, nh, bq, HEAD_DIM), lambda b, qi, ki: (b, 0, qi, 0))
    kspec = pl.BlockSpec((None, nh, bq, HEAD_DIM), lambda b, qi, ki: (b, 0, jnp.minimum(ki, qi), 0))
    return pl.pallas_call(
        _fox_kernel, grid=(B, nq, nq),
        in_specs=[qspec, qspec, kspec, kspec, kspec],
        out_specs=pl.BlockSpec((None, bq, nh * HEAD_DIM), lambda b, qi, ki: (b, qi, 0)),
        out_shape=jax.ShapeDtypeStruct((B, L, nh * HEAD_DIM), BF16),
        scratch_shapes=[pltpu.VMEM((nh, bq, 1), F32), pltpu.VMEM((nh, bq, 1), F32),
                        pltpu.VMEM((nh, bq, HEAD_DIM), F32)],
        compiler_params=_params(("parallel", "parallel", "arbitrary")), name="fox_prompt",
    )(q, qa, k, ka, v)


def _fox_decode_kernel(pt_ref, q_ref, lx_ref, kn_ref, vn_ref, *refs, pps, n_new):
    kp = refs[:pps]
    vp = refs[pps:2 * pps]
    lp = refs[2 * pps:3 * pps]
    o_ref, m_sc, l_sc, acc_sc, suf_sc = refs[3 * pps:]
    del pt_ref
    step = pl.program_id(1)
    nh, tp, _ = q_ref.shape
    page = kn_ref.shape[1]

    @pl.when(step == 0)
    def _():
        m_sc[...] = jnp.full(m_sc.shape, -jnp.inf, F32)
        l_sc[...] = jnp.zeros(l_sc.shape, F32)
        acc_sc[...] = jnp.zeros(acc_sc.shape, F32)
        suf_sc[...] = jnp.zeros(suf_sc.shape, F32)

    ck = lx_ref[...]
    col = lax.broadcasted_iota(jnp.int32, ck.shape, 1)
    row_t = lax.broadcasted_iota(jnp.int32, ck.shape, 0) % tp
    s = 1
    while s < n_new:
        ck = ck + jnp.where(col >= s, pltpu.roll(ck, s, axis=1), 0.0)
        s *= 2
    cq = jnp.sum(jnp.where(col == row_t, ck, 0.0), axis=-1, keepdims=True)

    def update(sc, v_tiles):
        m_prev = m_sc[...]
        m_new = jnp.maximum(m_prev, jnp.max(sc, axis=-1, keepdims=True))
        alpha = jnp.exp(m_prev - m_new)
        p = jnp.exp(sc - m_new)
        l_sc[...] = alpha * l_sc[...] + jnp.sum(p, axis=-1, keepdims=True)
        pv = []
        for h in range(nh):
            ph = p[h * tp:(h + 1) * tp].astype(BF16)
            acc = _dot(ph[:, :page], v_tiles[0](h))
            for i in range(1, len(v_tiles)):
                acc += _dot(ph[:, i * page:(i + 1) * page], v_tiles[i](h))
            pv.append(acc)
        acc_sc[...] = alpha * acc_sc[...] + jnp.concatenate(pv, axis=0)
        m_sc[...] = m_new

    lane = lax.broadcasted_iota(jnp.int32, (HEADS_PAD, LANES), 1)
    after = suf_sc[...]
    biases = []
    for i in range(pps):
        lf = lp[i][...]
        inc = lf
        s = 1
        while s < LANES:
            inc = inc + jnp.where(lane < LANES - s, pltpu.roll(inc, LANES - s, axis=1), 0.0)
            s *= 2
        biases.append(inc - lf + after)
        after = after + jnp.broadcast_to(inc[:, 0:1], after.shape)
    suf_sc[...] = after
    bias = jnp.concatenate(biases, axis=1)

    rows = []
    for h in range(nh):
        qh = q_ref[h]
        sh = jnp.concatenate([_dot_nt(qh, kp[i][h].astype(BF16)) for i in range(pps)], axis=1)
        rows.append(sh + bias[h:h + 1])
    update(jnp.concatenate(rows, axis=0) + cq,
           [lambda h, i=i: vp[i][h].astype(BF16) for i in range(pps)])

    @pl.when(step == pl.num_programs(1) - 1)
    def _():
        sc = jnp.concatenate([_dot_nt(q_ref[h], kn_ref[h]) for h in range(nh)], axis=0) + cq - ck
        sc = jnp.where((col <= row_t) & (col < n_new), sc, -jnp.inf)
        update(sc, [lambda h: vn_ref[h]])
        o_ref[...] = acc_sc[...] / l_sc[...]


def fox_decode(layer, page_table, q, lx, kn, vn, cache_k, cache_v, cache_lf, n_new):
    Bd, nh, tp, _ = q.shape
    n_pages = page_table.shape[1]
    page = cache_k.shape[3]
    pps = DECODE_PAGES_PER_STEP if n_pages % DECODE_PAGES_PER_STEP == 0 else 1
    steps = n_pages // pps
    R = nh * tp

    def page_spec(i, shape):
        zeros = (0,) * len(shape)
        return pl.BlockSpec((None, None) + shape,
                            lambda b, s, pt: (layer, pt[b, n_pages - 1 - (s * pps + i)]) + zeros)

    def per_b(shape):
        zeros = (0,) * len(shape)
        return pl.BlockSpec((None,) + shape, lambda b, s, pt: (b,) + zeros)

    in_specs = ([per_b((nh, tp, HEAD_DIM)), per_b((R, LANES)), per_b((nh, page, HEAD_DIM)),
                 per_b((nh, page, HEAD_DIM))]
                + [page_spec(i, (nh, page, HEAD_DIM)) for i in range(pps)]
                + [page_spec(i, (nh, page, HEAD_DIM)) for i in range(pps)]
                + [page_spec(i, (HEADS_PAD, page)) for i in range(pps)])
    grid_spec = pltpu.PrefetchScalarGridSpec(
        num_scalar_prefetch=1, grid=(Bd, steps), in_specs=in_specs,
        out_specs=per_b((R, HEAD_DIM)),
        scratch_shapes=[pltpu.VMEM((R, 1), F32), pltpu.VMEM((R, 1), F32), pltpu.VMEM((R, HEAD_DIM), F32),
                        pltpu.VMEM((HEADS_PAD, LANES), F32)])
    return pl.pallas_call(
        functools.partial(_fox_decode_kernel, pps=pps, n_new=n_new), grid_spec=grid_spec,
        out_shape=jax.ShapeDtypeStruct((Bd, R, HEAD_DIM), F32),
        compiler_params=_params(("parallel", "arbitrary")), name="fox_decode",
    )(page_table, q, lx, kn, vn, *([cache_k] * pps), *([cache_v] * pps), *([cache_lf] * pps))


def _s5_disc_kernel(are_ref, aim_ref, ldt_ref, bre_ref, bim_ref, abre_ref, abim_ref, bbre_ref, bbim_ref):
    a_re = are_ref[0]
    a_im = aim_ref[0]
    dt = jnp.exp(ldt_ref[0])
    mag = jnp.exp(a_re * dt)
    ab_re = mag * jnp.cos(a_im * dt)
    ab_im = mag * jnp.sin(a_im * dt)
    den = a_re * a_re + a_im * a_im
    n_re = ab_re - 1.0
    q_re = (n_re * a_re + ab_im * a_im) / den
    q_im = (ab_im * a_re - n_re * a_im) / den
    b_re = bre_ref[0]
    b_im = bim_ref[0]
    abre_ref[0] = ab_re
    abim_ref[0] = ab_im
    bbre_ref[0] = q_re * b_re - q_im * b_im
    bbim_ref[0] = q_re * b_im + q_im * b_re


def s5_discretize(a_re, a_im, log_dt, b_re_t, b_im_t):
    Dp, G, P = a_re.shape
    C = b_re_t.shape[2]
    gp = pl.BlockSpec((1, G, 1, P), lambda l: (l, 0, 0, 0))
    gcp = pl.BlockSpec((1, G, C, P), lambda l: (l, 0, 0, 0))
    return pl.pallas_call(
        _s5_disc_kernel, grid=(Dp,),
        in_specs=[gp, gp, pl.BlockSpec((1, G, 1, 1), lambda l: (l, 0, 0, 0)), gcp, gcp],
        out_specs=[gp, gp, gcp, gcp],
        out_shape=[jax.ShapeDtypeStruct((Dp, G, 1, P), F32)] * 2 + [jax.ShapeDtypeStruct((Dp, G, C, P), F32)] * 2,
        compiler_params=_params(("parallel",)), name="s5_discretize",
    )(a_re.reshape(Dp, G, 1, P), a_im.reshape(Dp, G, 1, P), log_dt.reshape(Dp, G, 1, 1), b_re_t, b_im_t)


def _s5_kernel(u_ref, h0re_ref, h0im_ref, are_ref, aim_ref, wbu_ref, wcre_ref, wcim_ref, d_ref, wglu_ref,
               o_ref, hre_ref, him_ref, bre_sc, bim_sc, *, nb, lane_chunk):
    n = pl.program_id(0)
    rows = u_ref.shape[0]
    gp = bre_sc.shape[1]
    kc = wbu_ref.shape[1]
    sc = wbu_ref.shape[2] // 2
    n_chunks = wbu_ref.shape[0]

    @pl.when(n == 0)
    def _():
        hre_ref[...] = h0re_ref[...]
        him_ref[...] = h0im_ref[...]

    u = u_ref[...]
    ub = u.astype(BF16)
    for c in range(n_chunks):
        bu = _dot(ub[:, c * kc:(c + 1) * kc], wbu_ref[c])
        bre_sc[:, c * sc:(c + 1) * sc] = bu[:, :sc]
        bim_sc[:, c * sc:(c + 1) * sc] = bu[:, sc:]

    spt = SUBLANES // nb
    for c in range(gp // lane_chunk):
        cs = slice(c * lane_chunk, (c + 1) * lane_chunk)
        ar = jnp.broadcast_to(are_ref[:, cs], (SUBLANES, lane_chunk))
        ai = jnp.broadcast_to(aim_ref[:, cs], (SUBLANES, lane_chunk))
        band = lax.broadcasted_iota(jnp.int32, (SUBLANES, lane_chunk), 0) // nb

        def step(t, carry, cs=cs, ar=ar, ai=ai, band=band):
            hr, hi = carry
            r0 = pl.multiple_of(t * SUBLANES, SUBLANES)
            xr = bre_sc[pl.ds(r0, SUBLANES), cs]
            xi = bim_sc[pl.ds(r0, SUBLANES), cs]
            out_r = out_i = None
            for k in range(spt):
                hr, hi = ar * hr - ai * hi + xr, ar * hi + ai * hr + xi
                out_r = hr if k == 0 else jnp.where(band == k, hr, out_r)
                out_i = hi if k == 0 else jnp.where(band == k, hi, out_i)
                if spt > 1:
                    hr = pltpu.roll(hr, nb, axis=0)
                    hi = pltpu.roll(hi, nb, axis=0)
            bre_sc[pl.ds(r0, SUBLANES), cs] = out_r
            bim_sc[pl.ds(r0, SUBLANES), cs] = out_i
            return hr, hi

        init = lambda ref: jnp.concatenate([ref[:, cs]] * spt, axis=0)
        hr, hi = lax.fori_loop(0, rows // SUBLANES, step, (init(hre_ref), init(him_ref)))
        hre_ref[:, cs] = hr[:nb]
        him_ref[:, cs] = hi[:nb]

    ys = []
    for c in range(n_chunks):
        hr = bre_sc[:, c * sc:(c + 1) * sc].astype(BF16)
        hi = bim_sc[:, c * sc:(c + 1) * sc].astype(BF16)
        ys.append(_dot(hr, wcre_ref[c]) - _dot(hi, wcim_ref[c]))
    y = jnp.concatenate(ys, axis=1) + d_ref[...] * u
    z = _gelu_tanh(y)
    o_ref[...] = (z * _sigmoid(_dot(z.astype(BF16), wglu_ref[...]))).astype(o_ref.dtype)


def s5_branch(u_tb, h0_re, h0_im, ab_re, ab_im, wbu, wc_re, wc_im, d_skip, w_glu, nb):
    rows_all, wb = u_tb.shape
    L = rows_all // nb
    gp = h0_re.shape[1]
    tc = _tile(L, 128)
    rows = tc * nb
    lane_chunk = _tile(gp, 768)
    full = lambda a: _resident(a.shape, lambda n: (0,) * a.ndim)
    st = pl.BlockSpec((nb, gp), lambda n: (0, 0))
    return pl.pallas_call(
        functools.partial(_s5_kernel, nb=nb, lane_chunk=lane_chunk), grid=(L // tc,),
        in_specs=[pl.BlockSpec((rows, wb), lambda n: (n, 0)), st, st, full(ab_re), full(ab_im),
                  full(wbu), full(wc_re), full(wc_im), full(d_skip), full(w_glu)],
        out_specs=[pl.BlockSpec((rows, wb), lambda n: (n, 0)), st, st],
        out_shape=[jax.ShapeDtypeStruct((rows_all, wb), BF16), jax.ShapeDtypeStruct((nb, gp), F32),
                   jax.ShapeDtypeStruct((nb, gp), F32)],
        scratch_shapes=[pltpu.VMEM((rows, gp), F32), pltpu.VMEM((rows, gp), F32)],
        compiler_params=_params(("arbitrary",)), name="s5_branch",
    )(u_tb, h0_re, h0_im, ab_re, ab_im, wbu, wc_re, wc_im, d_skip, w_glu)


def _mem_kv_kernel(x_ref, g_ref, w_ref, k_ref, v_ref, kb_ref, vb_ref):
    h = _rms(x_ref[...], g_ref[0]).astype(BF16)
    z = _dot(h, w_ref[0])
    wm = k_ref.shape[-1]
    k_ref[0] = z[:, :wm]
    v_ref[0] = z[:, wm:]
    kb_ref[0] = z[:, :wm].astype(BF16)
    vb_ref[0] = z[:, wm:].astype(BF16)


def mem_kv(mem, g_mem, w_mkv):
    R, D = mem.shape
    Dp, _, W2 = w_mkv.shape
    wm = W2 // 2
    bm = _tile(R, 512)
    o = pl.BlockSpec((1, bm, wm), lambda l, i: (l, i, 0))
    return pl.pallas_call(
        _mem_kv_kernel, grid=(Dp, R // bm),
        in_specs=[pl.BlockSpec((bm, D), lambda l, i: (i, 0)), pl.BlockSpec((1, 1, D), lambda l, i: (l, 0, 0)),
                  pl.BlockSpec((1, D, W2), lambda l, i: (l, 0, 0))],
        out_specs=[o, o, o, o],
        out_shape=[jax.ShapeDtypeStruct((Dp, R, wm), F32)] * 2 + [jax.ShapeDtypeStruct((Dp, R, wm), BF16)] * 2,
        compiler_params=_params(("parallel", "arbitrary")), name="mem_kv",
    )(mem, g_mem, w_mkv)


def _mem_attn_kernel(q_ref, k_ref, v_ref, o_ref):
    nh = q_ref.shape[-1] // HEAD_DIM
    for h in range(nh):
        hs = slice(h * HEAD_DIM, (h + 1) * HEAD_DIM)
        s = _dot_nt(q_ref[:, hs], k_ref[:, hs])
        p = jnp.exp(s - jnp.max(s, axis=-1, keepdims=True))
        p = p / jnp.sum(p, axis=-1, keepdims=True)
        o_ref[:, hs] = _dot(p.astype(BF16), v_ref[:, hs]).astype(o_ref.dtype)


def mem_attn(q, k, v):
    B, L, W = q.shape
    Nm = k.shape[1]
    bq = _tile(L, 512)
    qs = pl.BlockSpec((None, bq, W), lambda b, i: (b, i, 0))
    ks = pl.BlockSpec((None, Nm, W), lambda b, i: (b, 0, 0))
    return pl.pallas_call(
        _mem_attn_kernel, grid=(B, L // bq), in_specs=[qs, ks, ks], out_specs=qs,
        out_shape=jax.ShapeDtypeStruct((B, L, W), BF16),
        compiler_params=_params(("parallel", "parallel")), name="mem_attn",
    )(q, k, v)


def _merge_kernel(h_ref, oa_ref, ob_ref, om_ref, g0_ref, g1_ref, g2_ref, pa_ref, pb_ref, pm_ref, o_ref):
    h = h_ref[...]
    acc = _sigmoid(_dot(h, g0_ref[...])) * _dot(oa_ref[...], pa_ref[...])
    acc += _sigmoid(_dot(h, g1_ref[...])) * _dot(ob_ref[...], pb_ref[...])
    acc += _sigmoid(_dot(h, g2_ref[...])) * _dot(om_ref[...], pm_ref[...])
    o_ref[...] = acc.astype(o_ref.dtype)


def merge(h, oa, ob, om, w, layer, gate_off, pa, pb, pm):
    M, D = h.shape
    bm = _tile(M, 1024)
    bn = _tile(D, GATE_ALIGN)
    nj = D // bn
    g0 = gate_off // bn
    rowspec = lambda a: pl.BlockSpec((bm, a.shape[1]), lambda j, i: (i, 0))
    colspec = lambda a: pl.BlockSpec((a.shape[0], bn), lambda j, i: (0, j))
    gate = lambda b: pl.BlockSpec((None, D, bn), lambda j, i: (layer, 0, g0 + b * nj + j))
    return pl.pallas_call(
        _merge_kernel, grid=(nj, M // bm),
        in_specs=[rowspec(h), rowspec(oa), rowspec(ob), rowspec(om), gate(0), gate(1), gate(2),
                  colspec(pa), colspec(pb), colspec(pm)],
        out_specs=pl.BlockSpec((bm, bn), lambda j, i: (i, j)),
        out_shape=jax.ShapeDtypeStruct((M, D), BF16),
        compiler_params=_params(("parallel", "arbitrary")), name="merge",
    )(h, oa, ob, om, w, w, w, pa, pb, pm)


def _proj_res_kernel(a_ref, w_ref, x_ref, gpost_ref, gnext_ref, xo_ref, *h_ref):
    f = _dot(a_ref[...], w_ref[...])
    x = x_ref[...] + _rms(f, gpost_ref[...])
    xo_ref[...] = x
    if h_ref:
        h_ref[0][...] = _rms(x, gnext_ref[...]).astype(BF16)


def proj_residual(a, w, x, g_post, g_next, emit_h=True):
    M, K = a.shape
    D = w.shape[1]
    bm = _tile(M, 256)
    row = lambda n: pl.BlockSpec((bm, n), lambda i: (i, 0))
    g = pl.BlockSpec((1, D), lambda i: (0, 0))
    out_specs = [row(D)] + ([row(D)] if emit_h else [])
    out_shape = [jax.ShapeDtypeStruct((M, D), F32)] + ([jax.ShapeDtypeStruct((M, D), BF16)] if emit_h else [])
    res = pl.pallas_call(
        _proj_res_kernel, grid=(M // bm,),
        in_specs=[row(K), _resident((K, D), lambda i: (0, 0)), row(D), g, g],
        out_specs=out_specs, out_shape=out_shape,
        compiler_params=_params(("parallel",)), name="proj_residual",
    )(a, w, x, g_post.reshape(1, D), g_next.reshape(1, D))
    return (res[0], res[1]) if emit_h else (res[0], None)


def _ffn_up_kernel(h_ref, wa_ref, wb_ref, cwa_ref, cwb_ref, cba_ref, cbb_ref, act_ref, sa_ref, sb_ref,
                   wa_sc, wb_sc, ca_sc, cb_sc, *, tiles_per_seq, sub):
    i = pl.program_id(1)
    bm = h_ref.shape[0]

    @pl.when(i == 0)
    def _():
        wa_sc[...] = wa_ref[...].astype(BF16)
        wb_sc[...] = wb_ref[...].astype(BF16)

    @pl.when(i % tiles_per_seq == 0)
    def _():
        ca_sc[...] = jnp.zeros(ca_sc.shape, F32)
        cb_sc[...] = jnp.zeros(cb_sc.shape, F32)

    def conv(u, prev, cw_ref, cb_ref):
        row = lax.broadcasted_iota(jnp.int32, u.shape, 0)
        u1 = jnp.where(row == 0, prev[SUBLANES - 1:SUBLANES], pltpu.roll(u, 1, axis=0))
        u2 = jnp.where(row == 0, prev[SUBLANES - 2:SUBLANES - 1],
                       jnp.where(row == 1, prev[SUBLANES - 1:SUBLANES], pltpu.roll(u, 2, axis=0)))
        return cb_ref[...] + cw_ref[2:3] * u + cw_ref[1:2] * u1 + cw_ref[0:1] * u2

    prev_a = ca_sc[...]
    prev_b = cb_sc[...]
    for r in range(bm // sub):
        rows = slice(r * sub, (r + 1) * sub)
        h = h_ref[rows, :]
        ua = _dot(h, wa_sc[...])
        ub = _dot(h, wb_sc[...])
        ya = conv(ua, prev_a, cwa_ref, cba_ref)
        yb = conv(ub, prev_b, cwb_ref, cbb_ref)
        act_ref[rows, :] = (_gelu_tanh(ya) * yb).astype(act_ref.dtype)
        prev_a = ua[sub - SUBLANES:]
        prev_b = ub[sub - SUBLANES:]
    ca_sc[...] = prev_a
    cb_sc[...] = prev_b
    sa_ref[...] = prev_a[SUBLANES - 2:]
    sb_ref[...] = prev_b[SUBLANES - 2:]


def ffn_up(h, w_up, layer, conv_w, conv_b, n_seq):
    M, D = h.shape
    F2 = w_up.shape[2]
    F = F2 // 2
    L = M // n_seq
    bm = _tile(L, 1024)
    bn = _tile(F, 512)
    sub = bm
    nj = F // bn
    tps = L // bm
    a_col = lambda r: pl.BlockSpec((r, bn), lambda j, i: (0, j))
    b_col = lambda r: pl.BlockSpec((r, bn), lambda j, i: (0, nj + j))
    wa = pl.BlockSpec((None, D, bn), lambda j, i: (layer, 0, j))
    wb = pl.BlockSpec((None, D, bn), lambda j, i: (layer, 0, nj + j))
    st = pl.BlockSpec((None, 2, bn), lambda j, i: (i // tps, 0, j))
    return pl.pallas_call(
        functools.partial(_ffn_up_kernel, tiles_per_seq=tps, sub=sub), grid=(nj, M // bm),
        in_specs=[pl.BlockSpec((bm, D), lambda j, i: (i, 0)), wa, wb, a_col(3), b_col(3), a_col(1), b_col(1)],
        out_specs=[pl.BlockSpec((bm, bn), lambda j, i: (i, j)), st, st],
        out_shape=[jax.ShapeDtypeStruct((M, F), BF16), jax.ShapeDtypeStruct((n_seq, 2, F), F32),
                   jax.ShapeDtypeStruct((n_seq, 2, F), F32)],
        scratch_shapes=[pltpu.VMEM((D, bn), BF16), pltpu.VMEM((D, bn), BF16),
                        pltpu.VMEM((SUBLANES, bn), F32), pltpu.VMEM((SUBLANES, bn), F32)],
        compiler_params=_params(("parallel", "arbitrary")), name="ffn_up",
    )(h, w_up, w_up, conv_w, conv_w, conv_b, conv_b)


def _ffn_up_step_kernel(h_ref, wa_ref, wb_ref, sta_ref, stb_ref, cwa_ref, cwb_ref, cba_ref, cbb_ref,
                        act_ref, sa_ref, sb_ref, *, n_seq):
    h = h_ref[...]
    T = h.shape[0] // n_seq

    def conv(w_ref, st_ref, cw_ref, cb_ref, s_ref):
        u = _dot(h, w_ref[...].astype(BF16))
        ext = jnp.concatenate([st_ref[...], u], axis=0)
        s_ref[...] = ext[T * n_seq:]
        return (cb_ref[...] + cw_ref[2:3] * u + cw_ref[1:2] * ext[n_seq:(T + 1) * n_seq]
                + cw_ref[0:1] * ext[:T * n_seq])

    ya = conv(wa_ref, sta_ref, cwa_ref, cba_ref, sa_ref)
    yb = conv(wb_ref, stb_ref, cwb_ref, cbb_ref, sb_ref)
    act_ref[...] = (_gelu_tanh(ya) * yb).astype(act_ref.dtype)


def ffn_up_step(h_tb, w_up, layer, conv_w, conv_b, state_tb, n_seq):
    M, D = h_tb.shape
    F2 = w_up.shape[2]
    F = F2 // 2
    bn = _tile(F, 512)
    nj = F // bn
    a_col = lambda r: pl.BlockSpec((r, bn), lambda j: (0, j))
    b_col = lambda r: pl.BlockSpec((r, bn), lambda j: (0, nj + j))
    wa = pl.BlockSpec((None, D, bn), lambda j: (layer, 0, j))
    wb = pl.BlockSpec((None, D, bn), lambda j: (layer, 0, nj + j))
    return pl.pallas_call(
        functools.partial(_ffn_up_step_kernel, n_seq=n_seq), grid=(nj,),
        in_specs=[pl.BlockSpec((M, D), lambda j: (0, 0)), wa, wb, a_col(2 * n_seq), b_col(2 * n_seq),
                  a_col(3), b_col(3), a_col(1), b_col(1)],
        out_specs=[a_col(M), a_col(2 * n_seq), a_col(2 * n_seq)],
        out_shape=[jax.ShapeDtypeStruct((M, F), BF16), jax.ShapeDtypeStruct((2 * n_seq, F), F32),
                   jax.ShapeDtypeStruct((2 * n_seq, F), F32)],
        compiler_params=_params(("parallel",)), name="ffn_up_step",
    )(h_tb, w_up, w_up, state_tb, state_tb, conv_w, conv_w, conv_b, conv_b)


def _s5_weights(bb_re_t, bb_im_t, c_re, c_im):
    G, C, P = bb_re_t.shape
    gc = S5_GROUPS_PER_CHUNK if G % S5_GROUPS_PER_CHUNK == 0 else G
    n = G // gc
    eye = jnp.eye(gc, dtype=F32)

    def bu(bb):
        return jnp.einsum('ngcp,gh->ngchp', bb.reshape(n, gc, C, P), eye).reshape(n, gc * C, gc * P)

    def cy(c):
        return jnp.einsum('ngcp,gh->ngphc', c.reshape(n, gc, C, P), eye).reshape(n, gc * P, gc * C)

    wbu = jnp.concatenate([bu(bb_re_t), bu(bb_im_t)], axis=-1).astype(BF16)
    return wbu, cy(c_re).astype(BF16), cy(c_im).astype(BF16)


def _layer_tokens(x, h, wl, n_seq, q_scale, branch_fn, ffn_fn, emit_h):
    q, k, v, kb, vb, u, qm, lf = inproj(h, wl['w_pad'], wl['layer'], wl['n_proj'], wl['b_f'], wl['wa'], wl['wb'],
                                        wl['wm'], wl['nh'], n_seq, q_scale)
    o_a, o_b, o_m, extra = branch_fn(q, kb, vb, lf, u, qm)
    merged = merge(h, o_a, o_b, o_m, wl['w_pad'], wl['layer'], wl['gate_off'], wl['p_a'], wl['p_b'], wl['p_m'])
    x, h2 = proj_residual(merged, wl['w_out'], x, wl['g_post_mix'], wl['g_pre_ffn'])
    act, conv_new = ffn_fn(h2)
    x, h_next = proj_residual(act, wl['w_down'], x, wl['g_post_ffn'], wl['g_next'], emit_h=emit_h)
    return x, h_next, k, v, lf, conv_new, extra


def kernel(x_prompt, x_sample, cache_k, cache_v, cache_logf, cache_mem_k, cache_mem_v, state_ssm_re, state_ssm_im, state_conv, page_table, mem_prompt, w_in, b_f, a_re, a_im, log_dt, b_re, b_im, c_re, c_im, d_skip, w_glu, g_mem, w_mkv, p_a, p_b, p_m, w_out, g_pre_mix, g_post_mix, g_pre_ffn, g_post_ffn, w_up, conv_w, conv_b, w_down):
    Bp, L, D = x_prompt.shape
    Bd, T, _ = x_sample.shape
    depth = w_in.shape[0]
    wa = p_a.shape[1]
    wb = p_b.shape[1]
    wm = p_m.shape[1]
    nh = b_f.shape[1]
    nhm = wm // HEAD_DIM
    G, P = a_re.shape[1:]
    C = b_re.shape[3]
    gp = G * P
    n_mem = mem_prompt.shape[1]
    n_pool, page = cache_k.shape[1:3]
    F2 = w_up.shape[2]

    o_fl = 3 * wa
    o_u = o_fl + nh
    o_g = o_u + wb + wm
    n_proj = o_fl + LANES + wb + wm
    gate_off = -(-n_proj // GATE_ALIGN) * GATE_ALIGN
    zcols = lambda n: jnp.zeros((depth, D, n), w_in.dtype)
    w_pad = jnp.concatenate([w_in[:, :, :o_u], zcols(LANES - nh), w_in[:, :, o_u:o_g], zcols(gate_off - n_proj),
                             w_in[:, :, o_g:]], axis=-1).astype(BF16)
    b_f_pad = jnp.pad(b_f, ((0, 0), (0, LANES - nh))).reshape(depth, 1, LANES)
    ab_re, ab_im, bb_re_t, bb_im_t = s5_discretize(a_re, a_im, log_dt, b_re.transpose(0, 1, 3, 2),
                                                   b_im.transpose(0, 1, 3, 2))
    bf = lambda a: a.astype(BF16)
    w_mkv_b, p_a_b, p_b_b, p_m_b, w_out_b, w_down_b, w_glu_b = map(
        bf, (w_mkv, p_a, p_b, p_m, w_out, w_down, w_glu))

    mk, mv, mkb, mvb = mem_kv(mem_prompt.reshape(Bp * n_mem, D), g_mem.reshape(depth, 1, D), w_mkv_b)

    cache_k2 = cache_k.transpose(0, 1, 3, 2, 4)
    cache_v2 = cache_v.transpose(0, 1, 3, 2, 4)
    cache_lf_t = jnp.pad(cache_logf.transpose(0, 1, 3, 2), ((0, 0), (0, 0), (0, HEADS_PAD - nh), (0, 0)))
    cache_mk_b = bf(cache_mem_k).reshape(depth, Bd, n_mem, wm)
    cache_mv_b = bf(cache_mem_v).reshape(depth, Bd, n_mem, wm)

    xp = x_prompt.reshape(Bp * L, D)
    xs = x_sample.reshape(Bd * T, D)
    hp = rmsnorm_bf16(xp, g_pre_mix[0])
    hs = rmsnorm_bf16(xs, g_pre_mix[0])
    zeros_p = jnp.zeros((Bp, gp), F32)

    outs = [[] for _ in range(14)]
    for l in range(depth):
        wbu, wc_re, wc_im = _s5_weights(bb_re_t[l], bb_im_t[l], c_re[l], c_im[l])
        wl = dict(w_pad=w_pad, layer=l, n_proj=n_proj, gate_off=gate_off, b_f=b_f_pad[l], wa=wa, wb=wb, wm=wm, nh=nh,
                  p_a=p_a_b[l], p_b=p_b_b[l], p_m=p_m_b[l], w_out=w_out_b[l], w_down=w_down_b[l],
                  g_post_mix=g_post_mix[l], g_pre_ffn=g_pre_ffn[l], g_post_ffn=g_post_ffn[l],
                  g_next=g_pre_mix[(l + 1) % depth])
        s5_args = (ab_re[l].reshape(1, gp), ab_im[l].reshape(1, gp), wbu, wc_re, wc_im,
                   d_skip[l].reshape(1, wb), w_glu_b[l])
        emit_h = l + 1 < depth

        def prompt_branches(q, kb, vb, lf, u, qm):
            c = cumsum_lanes(lf.reshape(Bp, L, nh).transpose(0, 2, 1))
            qa, ka = fox_bias_features(c)
            o_a = fox_prompt(q, qa, kb, ka, vb).reshape(Bp * L, wa)
            u_tb = u.reshape(Bp, L, wb).transpose(1, 0, 2).reshape(L * Bp, wb)
            o_b, hre, him = s5_branch(u_tb, zeros_p, zeros_p, *s5_args, nb=Bp)
            o_b = o_b.reshape(L, Bp, wb).transpose(1, 0, 2).reshape(Bp * L, wb)
            o_m = mem_attn(qm.reshape(Bp, L, wm), mkb[l].reshape(Bp, n_mem, wm),
                           mvb[l].reshape(Bp, n_mem, wm)).reshape(Bp * L, wm)
            return o_a, o_b, o_m, (hre, him)

        def prompt_ffn(h2):
            act, sa, sb = ffn_up(h2, w_up, l, conv_w[l], conv_b[l].reshape(1, F2), Bp)
            return act, jnp.concatenate([sa, sb], axis=-1)

        xp, hp, k, v, lf, cv, (hre, him) = _layer_tokens(xp, hp, wl, Bp, HEAD_DIM ** -0.5 * LOG2E,
                                                         prompt_branches, prompt_ffn, emit_h)
        for i, a in zip((0, 1, 2, 3, 4, 5, 6, 7), (
                k, v, lf.reshape(Bp, L, nh),
                mk[l].reshape(Bp, n_mem, nhm, HEAD_DIM), mv[l].reshape(Bp, n_mem, nhm, HEAD_DIM),
                hre.reshape(Bp, G, P), him.reshape(Bp, G, P), cv)):
            outs[i].append(a)

        def sample_branches(q, kb, vb, lf, u, qm):
            tp = -(-T // SUBLANES) * SUBLANES
            padrows = lambda a, n: jnp.pad(a, ((0, 0), (0, 0), (0, n - T), (0, 0)))
            lx = jnp.pad(lf.reshape(Bd, T, nh).transpose(0, 2, 1), ((0, 0), (0, 0), (0, LANES - T)))
            lx = jnp.broadcast_to(lx[:, :, None], (Bd, nh, tp, LANES)).reshape(Bd, nh * tp, LANES)
            o = fox_decode(l, page_table, padrows(q, tp), lx, padrows(kb, page), padrows(vb, page),
                           cache_k2, cache_v2, cache_lf_t, T)
            o_a = o.reshape(Bd, nh, tp, HEAD_DIM)[:, :, :T].transpose(0, 2, 1, 3).reshape(Bd * T, wa).astype(BF16)
            u_tb = u.reshape(Bd, T, wb).transpose(1, 0, 2).reshape(T * Bd, wb)
            o_b, hre, him = s5_branch(u_tb, state_ssm_re[l].reshape(Bd, gp), state_ssm_im[l].reshape(Bd, gp),
                                      *s5_args, nb=Bd)
            o_b = o_b.reshape(T, Bd, wb).transpose(1, 0, 2).reshape(Bd * T, wb)
            o_m = mem_attn(qm.reshape(Bd, T, wm), cache_mk_b[l], cache_mv_b[l]).reshape(Bd * T, wm)
            return o_a, o_b, o_m, (hre, him)

        def sample_ffn(h2):
            h_tb = h2.reshape(Bd, T, D).transpose(1, 0, 2).reshape(T * Bd, D)
            st_tb = state_conv[l].transpose(1, 0, 2).reshape(2 * Bd, F2)
            act, sa, sb = ffn_up_step(h_tb, w_up, l, conv_w[l], conv_b[l].reshape(1, F2), st_tb, Bd)
            act = act.reshape(T, Bd, F2 // 2).transpose(1, 0, 2).reshape(Bd * T, F2 // 2)
            cv = jnp.concatenate([sa, sb], axis=-1).reshape(2, Bd, F2).transpose(1, 0, 2)
            return act, cv

        xs, hs, k, v, lf, cv, (hre, him) = _layer_tokens(xs, hs, wl, Bd, HEAD_DIM ** -0.5,
                                                         sample_branches, sample_ffn, emit_h)
        for i, a in zip((8, 9, 10, 11, 12, 13), (
                k, v, lf.reshape(Bd, T, nh),
                hre.reshape(Bd, G, P), him.reshape(Bd, G, P), cv)):
            outs[i].append(a)

    res = [jnp.stack(o) for o in outs]
    for i in (0, 1, 8, 9):
        res[i] = res[i].transpose(0, 1, 3, 2, 4)
    return (xp.reshape(Bp, L, D), xs.reshape(Bd, T, D)) + tuple(res)
```

```python
import functools
import math

import jax
import jax.numpy as jnp
from jax import lax
from jax.experimental import pallas as pl
from jax.experimental.pallas import tpu as pltpu

F32 = jnp.float32
BF16 = jnp.bfloat16

RMS_EPS = 1e-6
HEAD_DIM = 128
GROUP_CH = 16
LANES = 128
SUBLANES = 8
MXU_DIM = 256
VMEM_LIMIT = 56 * 1024 * 1024
S5_GROUPS_PER_CHUNK = MXU_DIM // GROUP_CH
HEADS_PAD = SUBLANES
LOG2E = 1.4426950408889634
DECODE_PAGES_PER_STEP = 16


def _tile(n, pref):
    return pref if n % pref == 0 else n


def _params(sem, vmem=VMEM_LIMIT):
    return pltpu.CompilerParams(dimension_semantics=sem, vmem_limit_bytes=vmem)


def _resident(shape, index_map):
    return pl.BlockSpec(shape, index_map, pipeline_mode=pl.Buffered(1))


def _dot(a, b):
    return jnp.dot(a, b, preferred_element_type=F32)


def _dot_nt(a, b):
    return lax.dot_general(a, b, (((1,), (1,)), ((), ())), preferred_element_type=F32)


def _rms(x, g):
    return x * lax.rsqrt(jnp.mean(x * x, axis=-1, keepdims=True) + RMS_EPS) * g


def _sigmoid(x):
    return 1.0 / (1.0 + jnp.exp(-x))


def _log_sigmoid(x):
    return jnp.minimum(x, 0.0) - jnp.log1p(jnp.exp(-jnp.abs(x)))


def _gelu_tanh(x):
    c = math.sqrt(2.0 / math.pi)
    return 0.5 * x * (1.0 + jnp.tanh(c * (x + 0.044715 * (x * x * x))))


def _rmsnorm_kernel(x_ref, g_ref, o_ref):
    o_ref[...] = _rms(x_ref[...], g_ref[...]).astype(o_ref.dtype)


def rmsnorm_bf16(x, g):
    M, D = x.shape
    bm = _tile(M, 512)
    return pl.pallas_call(
        _rmsnorm_kernel, grid=(M // bm,),
        in_specs=[pl.BlockSpec((bm, D), lambda i: (i, 0)), pl.BlockSpec((1, D), lambda i: (0, 0))],
        out_specs=pl.BlockSpec((bm, D), lambda i: (i, 0)),
        out_shape=jax.ShapeDtypeStruct((M, D), BF16),
        compiler_params=_params(("parallel",)), name="rmsnorm",
    )(x, g.reshape(1, D))


def _inproj_kernel(h_ref, wqkvf_ref, wum_ref, bf_ref, q_ref, k_ref, v_ref, kb_ref, vb_ref, u_ref, qm_ref, lf_ref,
                   *, wa, wb, wm, nh, q_scale):
    h = h_ref[...]
    scale = HEAD_DIM ** -0.5
    n_seq, _, seq_rows, _ = q_ref.shape

    def proj(a, b, w_ref=wqkvf_ref):
        return _dot(h, w_ref[:, a:b])

    def put_heads(ref, val):
        for s in range(n_seq):
            for hd in range(nh):
                ref[s, hd] = val[s * seq_rows:(s + 1) * seq_rows, hd * HEAD_DIM:(hd + 1) * HEAD_DIM]

    put_heads(q_ref, (proj(0, wa) * q_scale).astype(BF16))
    k = proj(wa, 2 * wa)
    put_heads(k_ref, k)
    put_heads(kb_ref, k.astype(BF16))
    v = proj(2 * wa, 3 * wa)
    put_heads(v_ref, v)
    put_heads(vb_ref, v.astype(BF16))
    o = 3 * wa
    fl = proj(o, o + LANES) + bf_ref[...]
    lf_ref[...] = _log_sigmoid(fl)[:, :nh]
    u_ref[...] = proj(0, wb, wum_ref)
    qm_ref[...] = (proj(wb, wb + wm, wum_ref) * scale).astype(BF16)


def _inproj_fill_kernel(h_ref, wqkvf_ref, wum_ref, bf_ref, kall_ref, vall_ref, *out_refs, **kw):
    del kall_ref, vall_ref
    _inproj_kernel(h_ref, wqkvf_ref, wum_ref, bf_ref, *out_refs, **kw)


def inproj(h, w_qkvf, w_um, layer, bf, wa, wb, wm, nh, n_seq, q_scale, kv_all):
    M, D = h.shape
    depth = w_qkvf.shape[0]
    L = M // n_seq
    bm = _tile(L, 512) if L >= SUBLANES else M
    tiles_per_seq = max(L // bm, 1)
    seqs_per_tile = max(bm // L, 1)
    rows_per_seq = min(bm, L)
    row = lambda n: pl.BlockSpec((bm, n), lambda i: (i, 0))
    heads = pl.BlockSpec((seqs_per_tile, nh, rows_per_seq, HEAD_DIM),
                         lambda i: (i // tiles_per_seq, 0, i % tiles_per_seq, 0))
    slab = pl.BlockSpec((None, seqs_per_tile, nh, rows_per_seq, HEAD_DIM),
                        lambda i: (layer, i // tiles_per_seq, 0, i % tiles_per_seq, 0))
    hshape = lambda dt: jax.ShapeDtypeStruct((n_seq, nh, L, HEAD_DIM), dt)
    all_shape = jax.ShapeDtypeStruct((depth, n_seq, nh, L, HEAD_DIM), F32)
    in_specs = [row(D), _resident((None, D, w_qkvf.shape[2]), lambda i: (layer, 0, 0)),
                _resident((None, D, w_um.shape[2]), lambda i: (layer, 0, 0)),
                pl.BlockSpec((1, LANES), lambda i: (0, 0))]
    kw = dict(wa=wa, wb=wb, wm=wm, nh=nh, q_scale=q_scale)
    if kv_all is None:
        body, extra, aliases = functools.partial(_inproj_kernel, **kw), (), {}
    else:
        body, extra, aliases = functools.partial(_inproj_fill_kernel, **kw), tuple(kv_all), {4: 1, 5: 2}
        in_specs += [pl.BlockSpec(memory_space=pl.ANY)] * 2
    return pl.pallas_call(
        body, grid=(M // bm,), in_specs=in_specs,
        out_specs=[heads, slab, slab, heads, heads, row(wb), row(wm), row(nh)],
        out_shape=[hshape(BF16), all_shape, all_shape, hshape(BF16), hshape(BF16),
                   jax.ShapeDtypeStruct((M, wb), F32), jax.ShapeDtypeStruct((M, wm), BF16),
                   jax.ShapeDtypeStruct((M, nh), F32)],
        input_output_aliases=aliases,
        compiler_params=_params(("parallel",)), name="inproj",
    )(h, w_qkvf, w_um, bf, *extra)


def _fox_bias_kernel(x_ref, qa_ref, ka_ref):
    x = x_ref[0]
    L = x.shape[-1]
    nh = qa_ref.shape[1]
    lane = lax.broadcasted_iota(jnp.int32, x.shape, 1)
    s = 1
    while s < L:
        x = x + jnp.where(lane >= s, pltpu.roll(x, s, axis=1), 0.0)
        s *= 2
    t = jnp.transpose(jnp.concatenate([x * LOG2E, jnp.zeros((LANES - x.shape[0], L), F32)], axis=0))

    def top_half(v):
        bits = lax.bitcast_convert_type(v, jnp.uint32) & jnp.uint32(0xFFFF0000)
        return lax.bitcast_convert_type(bits, F32)

    col = lax.broadcasted_iota(jnp.int32, t.shape, 1)
    for h in range(nh):
        c = jnp.broadcast_to(t[:, h:h + 1], t.shape)
        hi = top_half(c)
        mid = top_half(c - hi)
        lo = c - hi - mid
        ones = lambda a, b: jnp.where((col >= a) & (col < b), 1.0, 0.0)
        qa = jnp.where(col == 0, hi, jnp.where(col == 1, mid, jnp.where(col == 2, lo, ones(3, 6))))
        ka = jnp.where(col == 3, -hi, jnp.where(col == 4, -mid, jnp.where(col == 5, -lo, ones(0, 3))))
        qa_ref[0, h] = qa.astype(BF16)
        ka_ref[0, h] = ka.astype(BF16)


def fox_bias_features(logf, nh):
    B, HP, L = logf.shape
    out = pl.BlockSpec((1, nh, L, HEAD_DIM), lambda b: (b, 0, 0, 0))
    return pl.pallas_call(
        _fox_bias_kernel, grid=(B,),
        in_specs=[pl.BlockSpec((1, HP, L), lambda b: (b, 0, 0))],
        out_specs=[out, out],
        out_shape=[jax.ShapeDtypeStruct((B, nh, L, HEAD_DIM), BF16)] * 2,
        compiler_params=_params(("parallel",)), name="fox_bias",
    )(logf)


def _fox_kernel(q_ref, qa_ref, k_ref, ka_ref, v_ref, o_ref, m_sc, l_sc, acc_sc):
    qi = pl.program_id(1)
    ki = pl.program_id(2)
    nh = q_ref.shape[0]

    @pl.when(ki == 0)
    def _():
        m_sc[...] = jnp.full(m_sc.shape, -jnp.inf, F32)
        l_sc[...] = jnp.zeros(l_sc.shape, F32)
        acc_sc[...] = jnp.zeros(acc_sc.shape, F32)

    def update(masked):
        heads = range(nh)
        s = [_dot_nt(jnp.concatenate([q_ref[h], qa_ref[h]], axis=1),
                     jnp.concatenate([k_ref[h], ka_ref[h]], axis=1)) for h in heads]
        if masked:
            row = lax.broadcasted_iota(jnp.int32, s[0].shape, 0)
            col = lax.broadcasted_iota(jnp.int32, s[0].shape, 1)
            s = [jnp.where(col <= row, sh, -jnp.inf) for sh in s]
        m_prev = [m_sc[h] for h in heads]
        m_new = [jnp.maximum(m_prev[h], jnp.max(s[h], axis=-1, keepdims=True)) for h in heads]
        alpha = [jnp.exp2(m_prev[h] - m_new[h]) for h in heads]
        p = [jnp.exp2(s[h] - m_new[h]) for h in heads]
        for h in heads:
            l_sc[h] = alpha[h] * l_sc[h] + jnp.sum(p[h], axis=-1, keepdims=True)
            m_sc[h] = m_new[h]
        pv = [_dot(p[h].astype(BF16), v_ref[h]) for h in heads]
        for h in heads:
            acc_sc[h] = alpha[h] * acc_sc[h] + pv[h]

    @pl.when(ki < qi)
    def _():
        update(False)

    @pl.when(ki == qi)
    def _():
        update(True)
        for h in range(nh):
            o_ref[:, h * HEAD_DIM:(h + 1) * HEAD_DIM] = (acc_sc[h] / l_sc[h]).astype(o_ref.dtype)


def fox_prompt(q, qa, k, ka, v):
    B, nh, L, _ = q.shape
    bq = _tile(L, 512)
    nq = L // bq
    qspec = pl.BlockSpec((None, nh, bq, HEAD_DIM), lambda b, qi, ki: (b, 0, qi, 0))
    kspec = pl.BlockSpec((None, nh, bq, HEAD_DIM), lambda b, qi, ki: (b, 0, jnp.minimum(ki, qi), 0))
    return pl.pallas_call(
        _fox_kernel, grid=(B, nq, nq),
        in_specs=[qspec, qspec, kspec, kspec, kspec],
        out_specs=pl.BlockSpec((None, bq, nh * HEAD_DIM), lambda b, qi, ki: (b, qi, 0)),
        out_shape=jax.ShapeDtypeStruct((B, L, nh * HEAD_DIM), BF16),
        scratch_shapes=[pltpu.VMEM((nh, bq, 1), F32), pltpu.VMEM((nh, bq, 1), F32),
                        pltpu.VMEM((nh, bq, HEAD_DIM), F32)],
        compiler_params=_params(("parallel", "parallel", "arbitrary")), name="fox_prompt",
    )(q, qa, k, ka, v)


def _fox_decode_kernel(pt_ref, q_ref, lx_ref, kn_ref, vn_ref, *refs, pps, n_new):
    kp = refs[:pps]
    vp = refs[pps:2 * pps]
    lp = refs[2 * pps:3 * pps]
    o_ref, m_sc, l_sc, acc_sc, suf_sc = refs[3 * pps:]
    del pt_ref
    step = pl.program_id(1)
    nh, tp, _ = q_ref.shape
    page = kn_ref.shape[1]

    @pl.when(step == 0)
    def _():
        m_sc[...] = jnp.full(m_sc.shape, -jnp.inf, F32)
        l_sc[...] = jnp.zeros(l_sc.shape, F32)
        acc_sc[...] = jnp.zeros(acc_sc.shape, F32)
        suf_sc[...] = jnp.zeros(suf_sc.shape, F32)

    ck = lx_ref[...]
    col = lax.broadcasted_iota(jnp.int32, ck.shape, 1)
    row_t = lax.broadcasted_iota(jnp.int32, ck.shape, 0) % tp
    s = 1
    while s < n_new:
        ck = ck + jnp.where(col >= s, pltpu.roll(ck, s, axis=1), 0.0)
        s *= 2
    cq = jnp.sum(jnp.where(col == row_t, ck, 0.0), axis=-1, keepdims=True)

    def update(sc, v_tiles):
        m_prev = m_sc[...]
        m_new = jnp.maximum(m_prev, jnp.max(sc, axis=-1, keepdims=True))
        alpha = jnp.exp(m_prev - m_new)
        p = jnp.exp(sc - m_new)
        l_sc[...] = alpha * l_sc[...] + jnp.sum(p, axis=-1, keepdims=True)
        pv = []
        for h in range(nh):
            ph = p[h * tp:(h + 1) * tp].astype(BF16)
            acc = _dot(ph[:, :page], v_tiles[0](h))
            for i in range(1, len(v_tiles)):
                acc += _dot(ph[:, i * page:(i + 1) * page], v_tiles[i](h))
            pv.append(acc)
        acc_sc[...] = alpha * acc_sc[...] + jnp.concatenate(pv, axis=0)
        m_sc[...] = m_new

    lane = lax.broadcasted_iota(jnp.int32, (HEADS_PAD, LANES), 1)
    after = suf_sc[...]
    biases = []
    for i in range(pps):
        lf = lp[i][...]
        inc = lf
        s = 1
        while s < LANES:
            inc = inc + jnp.where(lane < LANES - s, pltpu.roll(inc, LANES - s, axis=1), 0.0)
            s *= 2
        biases.append(inc - lf + after)
        after = after + jnp.broadcast_to(inc[:, 0:1], after.shape)
    suf_sc[...] = after
    bias = jnp.concatenate(biases, axis=1)

    rows = []
    for h in range(nh):
        qh = q_ref[h]
        sh = jnp.concatenate([_dot_nt(qh, kp[i][h].astype(BF16)) for i in range(pps)], axis=1)
        rows.append(sh + bias[h:h + 1])
    update(jnp.concatenate(rows, axis=0) + cq,
           [lambda h, i=i: vp[i][h].astype(BF16) for i in range(pps)])

    @pl.when(step == pl.num_programs(1) - 1)
    def _():
        sc = jnp.concatenate([_dot_nt(q_ref[h], kn_ref[h]) for h in range(nh)], axis=0) + cq - ck
        sc = jnp.where((col <= row_t) & (col < n_new), sc, -jnp.inf)
        update(sc, [lambda h: vn_ref[h]])
        o_ref[...] = acc_sc[...] / l_sc[...]


def fox_decode(layer, page_table, q, lx, kn, vn, cache_k, cache_v, cache_lf, n_new):
    Bd, nh, tp, _ = q.shape
    n_pages = page_table.shape[1]
    page = cache_k.shape[3]
    pps = DECODE_PAGES_PER_STEP if n_pages % DECODE_PAGES_PER_STEP == 0 else 1
    steps = n_pages // pps
    R = nh * tp

    def page_spec(i, shape):
        zeros = (0,) * len(shape)
        return pl.BlockSpec((None, None) + shape,
                            lambda b, s, pt: (layer, pt[b, n_pages - 1 - (s * pps + i)]) + zeros)

    def per_b(shape):
        zeros = (0,) * len(shape)
        return pl.BlockSpec((None,) + shape, lambda b, s, pt: (b,) + zeros)

    in_specs = ([per_b((nh, tp, HEAD_DIM)), per_b((R, LANES)), per_b((nh, page, HEAD_DIM)),
                 per_b((nh, page, HEAD_DIM))]
                + [page_spec(i, (nh, page, HEAD_DIM)) for i in range(pps)]
                + [page_spec(i, (nh, page, HEAD_DIM)) for i in range(pps)]
                + [page_spec(i, (HEADS_PAD, page)) for i in range(pps)])
    grid_spec = pltpu.PrefetchScalarGridSpec(
        num_scalar_prefetch=1, grid=(Bd, steps), in_specs=in_specs,
        out_specs=per_b((R, HEAD_DIM)),
        scratch_shapes=[pltpu.VMEM((R, 1), F32), pltpu.VMEM((R, 1), F32), pltpu.VMEM((R, HEAD_DIM), F32),
                        pltpu.VMEM((HEADS_PAD, LANES), F32)])
    return pl.pallas_call(
        functools.partial(_fox_decode_kernel, pps=pps, n_new=n_new), grid_spec=grid_spec,
        out_shape=jax.ShapeDtypeStruct((Bd, R, HEAD_DIM), F32),
        compiler_params=_params(("parallel", "arbitrary")), name="fox_decode",
    )(page_table, q, lx, kn, vn, *([cache_k] * pps), *([cache_v] * pps), *([cache_lf] * pps))


def _s5_disc_kernel(are_ref, aim_ref, ldt_ref, bre_ref, bim_ref, abre_ref, abim_ref, bbre_ref, bbim_ref):
    a_re = are_ref[0]
    a_im = aim_ref[0]
    dt = jnp.exp(ldt_ref[0])
    mag = jnp.exp(a_re * dt)
    ab_re = mag * jnp.cos(a_im * dt)
    ab_im = mag * jnp.sin(a_im * dt)
    den = a_re * a_re + a_im * a_im
    n_re = ab_re - 1.0
    q_re = (n_re * a_re + ab_im * a_im) / den
    q_im = (ab_im * a_re - n_re * a_im) / den
    b_re = bre_ref[0]
    b_im = bim_ref[0]
    abre_ref[0] = ab_re
    abim_ref[0] = ab_im
    bbre_ref[0] = q_re * b_re - q_im * b_im
    bbim_ref[0] = q_re * b_im + q_im * b_re


def s5_discretize(a_re, a_im, log_dt, b_re_t, b_im_t):
    Dp, G, P = a_re.shape
    C = b_re_t.shape[2]
    gp = pl.BlockSpec((1, G, 1, P), lambda l: (l, 0, 0, 0))
    gcp = pl.BlockSpec((1, G, C, P), lambda l: (l, 0, 0, 0))
    return pl.pallas_call(
        _s5_disc_kernel, grid=(Dp,),
        in_specs=[gp, gp, pl.BlockSpec((1, G, 1, 1), lambda l: (l, 0, 0, 0)), gcp, gcp],
        out_specs=[gp, gp, gcp, gcp],
        out_shape=[jax.ShapeDtypeStruct((Dp, G, 1, P), F32)] * 2 + [jax.ShapeDtypeStruct((Dp, G, C, P), F32)] * 2,
        compiler_params=_params(("parallel",)), name="s5_discretize",
    )(a_re.reshape(Dp, G, 1, P), a_im.reshape(Dp, G, 1, P), log_dt.reshape(Dp, G, 1, 1), b_re_t, b_im_t)


def _s5_kernel(u_ref, h0re_ref, h0im_ref, are_ref, aim_ref, wbu_ref, wcre_ref, wcim_ref, d_ref, wglu_ref,
               o_ref, hre_ref, him_ref, bre_sc, bim_sc, *, nb, lane_chunk):
    n = pl.program_id(0)
    rows = u_ref.shape[0]
    gp = bre_sc.shape[1]
    kc = wbu_ref.shape[1]
    sc = wbu_ref.shape[2] // 2
    n_chunks = wbu_ref.shape[0]

    @pl.when(n == 0)
    def _():
        hre_ref[...] = h0re_ref[...]
        him_ref[...] = h0im_ref[...]

    u = u_ref[...]
    ub = u.astype(BF16)
    for c in range(n_chunks):
        bu = _dot(ub[:, c * kc:(c + 1) * kc], wbu_ref[c])
        bre_sc[:, c * sc:(c + 1) * sc] = bu[:, :sc]
        bim_sc[:, c * sc:(c + 1) * sc] = bu[:, sc:]

    spt = SUBLANES // nb
    for c in range(gp // lane_chunk):
        cs = slice(c * lane_chunk, (c + 1) * lane_chunk)
        ar = jnp.broadcast_to(are_ref[:, cs], (SUBLANES, lane_chunk))
        ai = jnp.broadcast_to(aim_ref[:, cs], (SUBLANES, lane_chunk))
        band = lax.broadcasted_iota(jnp.int32, (SUBLANES, lane_chunk), 0) // nb

        def step(t, carry, cs=cs, ar=ar, ai=ai, band=band):
            hr, hi = carry
            r0 = pl.multiple_of(t * SUBLANES, SUBLANES)
            xr = bre_sc[pl.ds(r0, SUBLANES), cs]
            xi = bim_sc[pl.ds(r0, SUBLANES), cs]
            out_r = out_i = None
            for k in range(spt):
                hr, hi = ar * hr - ai * hi + xr, ar * hi + ai * hr + xi
                out_r = hr if k == 0 else jnp.where(band == k, hr, out_r)
                out_i = hi if k == 0 else jnp.where(band == k, hi, out_i)
                if spt > 1:
                    hr = pltpu.roll(hr, nb, axis=0)
                    hi = pltpu.roll(hi, nb, axis=0)
            bre_sc[pl.ds(r0, SUBLANES), cs] = out_r
            bim_sc[pl.ds(r0, SUBLANES), cs] = out_i
            return hr, hi

        init = lambda ref: jnp.concatenate([ref[:, cs]] * spt, axis=0)
        hr, hi = lax.fori_loop(0, rows // SUBLANES, step, (init(hre_ref), init(him_ref)))
        hre_ref[:, cs] = hr[:nb]
        him_ref[:, cs] = hi[:nb]

    ys = []
    for c in range(n_chunks):
        hr = bre_sc[:, c * sc:(c + 1) * sc].astype(BF16)
        hi = bim_sc[:, c * sc:(c + 1) * sc].astype(BF16)
        ys.append(_dot(hr, wcre_ref[c]) - _dot(hi, wcim_ref[c]))
    y = jnp.concatenate(ys, axis=1) + d_ref[...] * u
    z = _gelu_tanh(y)
    o_ref[...] = (z * _sigmoid(_dot(z.astype(BF16), wglu_ref[...]))).astype(o_ref.dtype)


def s5_branch(u_tb, h0_re, h0_im, ab_re, ab_im, wbu, wc_re, wc_im, d_skip, w_glu, nb):
    rows_all, wb = u_tb.shape
    L = rows_all // nb
    gp = h0_re.shape[1]
    tc = _tile(L, 128)
    rows = tc * nb
    lane_chunk = _tile(gp, 768)
    full = lambda a: _resident(a.shape, lambda n: (0,) * a.ndim)
    st = pl.BlockSpec((nb, gp), lambda n: (0, 0))
    return pl.pallas_call(
        functools.partial(_s5_kernel, nb=nb, lane_chunk=lane_chunk), grid=(L // tc,),
        in_specs=[pl.BlockSpec((rows, wb), lambda n: (n, 0)), st, st, full(ab_re), full(ab_im),
                  full(wbu), full(wc_re), full(wc_im), full(d_skip), full(w_glu)],
        out_specs=[pl.BlockSpec((rows, wb), lambda n: (n, 0)), st, st],
        out_shape=[jax.ShapeDtypeStruct((rows_all, wb), BF16), jax.ShapeDtypeStruct((nb, gp), F32),
                   jax.ShapeDtypeStruct((nb, gp), F32)],
        scratch_shapes=[pltpu.VMEM((rows, gp), F32), pltpu.VMEM((rows, gp), F32)],
        compiler_params=_params(("arbitrary",)), name="s5_branch",
    )(u_tb, h0_re, h0_im, ab_re, ab_im, wbu, wc_re, wc_im, d_skip, w_glu)


def _mem_kv_kernel(x_ref, g_ref, w_ref, k_ref, v_ref, kb_ref, vb_ref):
    h = _rms(x_ref[...], g_ref[0]).astype(BF16)
    z = _dot(h, w_ref[0])
    wm = k_ref.shape[-1]
    k_ref[0] = z[:, :wm]
    v_ref[0] = z[:, wm:]
    kb_ref[0] = z[:, :wm].astype(BF16)
    vb_ref[0] = z[:, wm:].astype(BF16)


def mem_kv(mem, g_mem, w_mkv):
    R, D = mem.shape
    Dp, _, W2 = w_mkv.shape
    wm = W2 // 2
    bm = _tile(R, 512)
    o = pl.BlockSpec((1, bm, wm), lambda l, i: (l, i, 0))
    return pl.pallas_call(
        _mem_kv_kernel, grid=(Dp, R // bm),
        in_specs=[pl.BlockSpec((bm, D), lambda l, i: (i, 0)), pl.BlockSpec((1, 1, D), lambda l, i: (l, 0, 0)),
                  pl.BlockSpec((1, D, W2), lambda l, i: (l, 0, 0))],
        out_specs=[o, o, o, o],
        out_shape=[jax.ShapeDtypeStruct((Dp, R, wm), F32)] * 2 + [jax.ShapeDtypeStruct((Dp, R, wm), BF16)] * 2,
        compiler_params=_params(("parallel", "arbitrary")), name="mem_kv",
    )(mem, g_mem, w_mkv)


def _mem_attn_kernel(q_ref, k_ref, v_ref, o_ref):
    nh = q_ref.shape[-1] // HEAD_DIM
    for h in range(nh):
        hs = slice(h * HEAD_DIM, (h + 1) * HEAD_DIM)
        s = _dot_nt(q_ref[:, hs], k_ref[:, hs])
        p = jnp.exp(s - jnp.max(s, axis=-1, keepdims=True))
        p = p / jnp.sum(p, axis=-1, keepdims=True)
        o_ref[:, hs] = _dot(p.astype(BF16), v_ref[:, hs]).astype(o_ref.dtype)


def mem_attn(q, k, v):
    B, L, W = q.shape
    Nm = k.shape[1]
    bq = _tile(L, 512)
    qs = pl.BlockSpec((None, bq, W), lambda b, i: (b, i, 0))
    ks = pl.BlockSpec((None, Nm, W), lambda b, i: (b, 0, 0))
    return pl.pallas_call(
        _mem_attn_kernel, grid=(B, L // bq), in_specs=[qs, ks, ks], out_specs=qs,
        out_shape=jax.ShapeDtypeStruct((B, L, W), BF16),
        compiler_params=_params(("parallel", "parallel")), name="mem_attn",
    )(q, k, v)


def _merge_kernel(h_ref, oa_ref, ob_ref, om_ref, g0_ref, g1_ref, g2_ref, pa_ref, pb_ref, pm_ref, o_ref):
    h = h_ref[...]
    acc = _sigmoid(_dot(h, g0_ref[...])) * _dot(oa_ref[...], pa_ref[...])
    acc += _sigmoid(_dot(h, g1_ref[...])) * _dot(ob_ref[...], pb_ref[...])
    acc += _sigmoid(_dot(h, g2_ref[...])) * _dot(om_ref[...], pm_ref[...])
    o_ref[...] = acc.astype(o_ref.dtype)


def merge(h, oa, ob, om, w, layer, pa, pb, pm):
    M, D = h.shape
    bm = _tile(M, 1024)
    bn = _tile(D, 512)
    nj = D // bn
    rowspec = lambda a: pl.BlockSpec((bm, a.shape[1]), lambda j, i: (i, 0))
    colspec = lambda a: pl.BlockSpec((a.shape[0], bn), lambda j, i: (0, j))
    gate = lambda b: pl.BlockSpec((None, D, bn), lambda j, i: (layer, 0, b * nj + j))
    return pl.pallas_call(
        _merge_kernel, grid=(nj, M // bm),
        in_specs=[rowspec(h), rowspec(oa), rowspec(ob), rowspec(om), gate(0), gate(1), gate(2),
                  colspec(pa), colspec(pb), colspec(pm)],
        out_specs=pl.BlockSpec((bm, bn), lambda j, i: (i, j)),
        out_shape=jax.ShapeDtypeStruct((M, D), BF16),
        compiler_params=_params(("parallel", "arbitrary")), name="merge",
    )(h, oa, ob, om, w, w, w, pa, pb, pm)


def _proj_res_kernel(a_ref, w_ref, x_ref, gpost_ref, gnext_ref, xo_ref, *h_ref):
    f = _dot(a_ref[...], w_ref[...])
    x = x_ref[...] + _rms(f, gpost_ref[...])
    xo_ref[...] = x
    if h_ref:
        h_ref[0][...] = _rms(x, gnext_ref[...]).astype(BF16)


def proj_residual(a, w, x, g_post, g_next, emit_h=True):
    M, K = a.shape
    D = w.shape[1]
    bm = _tile(M, 256)
    row = lambda n: pl.BlockSpec((bm, n), lambda i: (i, 0))
    g = pl.BlockSpec((1, D), lambda i: (0, 0))
    out_specs = [row(D)] + ([row(D)] if emit_h else [])
    out_shape = [jax.ShapeDtypeStruct((M, D), F32)] + ([jax.ShapeDtypeStruct((M, D), BF16)] if emit_h else [])
    res = pl.pallas_call(
        _proj_res_kernel, grid=(M // bm,),
        in_specs=[row(K), _resident((K, D), lambda i: (0, 0)), row(D), g, g],
        out_specs=out_specs, out_shape=out_shape,
        compiler_params=_params(("parallel",)), name="proj_residual",
    )(a, w, x, g_post.reshape(1, D), g_next.reshape(1, D))
    return (res[0], res[1]) if emit_h else (res[0], None)


def _ffn_up_kernel(h_ref, wa_ref, wb_ref, cwa_ref, cwb_ref, cba_ref, cbb_ref, act_ref, sa_ref, sb_ref,
                   wa_sc, wb_sc, ca_sc, cb_sc, *, tiles_per_seq, sub):
    i = pl.program_id(1)
    bm = h_ref.shape[0]

    @pl.when(i == 0)
    def _():
        wa_sc[...] = wa_ref[...].astype(BF16)
        wb_sc[...] = wb_ref[...].astype(BF16)

    @pl.when(i % tiles_per_seq == 0)
    def _():
        ca_sc[...] = jnp.zeros(ca_sc.shape, F32)
        cb_sc[...] = jnp.zeros(cb_sc.shape, F32)

    def conv(u, prev, cw_ref, cb_ref):
        row = lax.broadcasted_iota(jnp.int32, u.shape, 0)
        u1 = jnp.where(row == 0, prev[SUBLANES - 1:SUBLANES], pltpu.roll(u, 1, axis=0))
        u2 = jnp.where(row == 0, prev[SUBLANES - 2:SUBLANES - 1],
                       jnp.where(row == 1, prev[SUBLANES - 1:SUBLANES], pltpu.roll(u, 2, axis=0)))
        return cb_ref[...] + cw_ref[2:3] * u + cw_ref[1:2] * u1 + cw_ref[0:1] * u2

    prev_a = ca_sc[...]
    prev_b = cb_sc[...]
    for r in range(bm // sub):
        rows = slice(r * sub, (r + 1) * sub)
        h = h_ref[rows, :]
        ua = _dot(h, wa_sc[...])
        ub = _dot(h, wb_sc[...])
        ya = conv(ua, prev_a, cwa_ref, cba_ref)
        yb = conv(ub, prev_b, cwb_ref, cbb_ref)
        act_ref[rows, :] = (_gelu_tanh(ya) * yb).astype(act_ref.dtype)
        prev_a = ua[sub - SUBLANES:]
        prev_b = ub[sub - SUBLANES:]
    ca_sc[...] = prev_a
    cb_sc[...] = prev_b
    sa_ref[...] = prev_a[SUBLANES - 2:]
    sb_ref[...] = prev_b[SUBLANES - 2:]


def ffn_up(h, w_up, layer, conv_w, conv_b, n_seq):
    M, D = h.shape
    F2 = w_up.shape[2]
    F = F2 // 2
    L = M // n_seq
    bm = _tile(L, 1024)
    bn = _tile(F, 512)
    sub = bm
    nj = F // bn
    tps = L // bm
    a_col = lambda r: pl.BlockSpec((r, bn), lambda j, i: (0, j))
    b_col = lambda r: pl.BlockSpec((r, bn), lambda j, i: (0, nj + j))
    wa = pl.BlockSpec((None, D, bn), lambda j, i: (layer, 0, j))
    wb = pl.BlockSpec((None, D, bn), lambda j, i: (layer, 0, nj + j))
    st = pl.BlockSpec((None, 2, bn), lambda j, i: (i // tps, 0, j))
    return pl.pallas_call(
        functools.partial(_ffn_up_kernel, tiles_per_seq=tps, sub=sub), grid=(nj, M // bm),
        in_specs=[pl.BlockSpec((bm, D), lambda j, i: (i, 0)), wa, wb, a_col(3), b_col(3), a_col(1), b_col(1)],
        out_specs=[pl.BlockSpec((bm, bn), lambda j, i: (i, j)), st, st],
        out_shape=[jax.ShapeDtypeStruct((M, F), BF16), jax.ShapeDtypeStruct((n_seq, 2, F), F32),
                   jax.ShapeDtypeStruct((n_seq, 2, F), F32)],
        scratch_shapes=[pltpu.VMEM((D, bn), BF16), pltpu.VMEM((D, bn), BF16),
                        pltpu.VMEM((SUBLANES, bn), F32), pltpu.VMEM((SUBLANES, bn), F32)],
        compiler_params=_params(("parallel", "arbitrary")), name="ffn_up",
    )(h, w_up, w_up, conv_w, conv_w, conv_b, conv_b)


def _ffn_up_step_kernel(h_ref, wa_ref, wb_ref, sta_ref, stb_ref, cwa_ref, cwb_ref, cba_ref, cbb_ref,
                        act_ref, sa_ref, sb_ref, *, n_seq):
    h = h_ref[...]
    T = h.shape[0] // n_seq

    def conv(w_ref, st_ref, cw_ref, cb_ref, s_ref):
        u = _dot(h, w_ref[...].astype(BF16))
        ext = jnp.concatenate([st_ref[...], u], axis=0)
        s_ref[...] = ext[T * n_seq:]
        return (cb_ref[...] + cw_ref[2:3] * u + cw_ref[1:2] * ext[n_seq:(T + 1) * n_seq]
                + cw_ref[0:1] * ext[:T * n_seq])

    ya = conv(wa_ref, sta_ref, cwa_ref, cba_ref, sa_ref)
    yb = conv(wb_ref, stb_ref, cwb_ref, cbb_ref, sb_ref)
    act_ref[...] = (_gelu_tanh(ya) * yb).astype(act_ref.dtype)


def ffn_up_step(h_tb, w_up, layer, conv_w, conv_b, state_tb, n_seq):
    M, D = h_tb.shape
    F2 = w_up.shape[2]
    F = F2 // 2
    bn = _tile(F, 512)
    nj = F // bn
    a_col = lambda r: pl.BlockSpec((r, bn), lambda j: (0, j))
    b_col = lambda r: pl.BlockSpec((r, bn), lambda j: (0, nj + j))
    wa = pl.BlockSpec((None, D, bn), lambda j: (layer, 0, j))
    wb = pl.BlockSpec((None, D, bn), lambda j: (layer, 0, nj + j))
    return pl.pallas_call(
        functools.partial(_ffn_up_step_kernel, n_seq=n_seq), grid=(nj,),
        in_specs=[pl.BlockSpec((M, D), lambda j: (0, 0)), wa, wb, a_col(2 * n_seq), b_col(2 * n_seq),
                  a_col(3), b_col(3), a_col(1), b_col(1)],
        out_specs=[a_col(M), a_col(2 * n_seq), a_col(2 * n_seq)],
        out_shape=[jax.ShapeDtypeStruct((M, F), BF16), jax.ShapeDtypeStruct((2 * n_seq, F), F32),
                   jax.ShapeDtypeStruct((2 * n_seq, F), F32)],
        compiler_params=_params(("parallel",)), name="ffn_up_step",
    )(h_tb, w_up, w_up, state_tb, state_tb, conv_w, conv_w, conv_b, conv_b)


def _s5_weights(bb_re_t, bb_im_t, c_re, c_im):
    G, C, P = bb_re_t.shape
    gc = S5_GROUPS_PER_CHUNK if G % S5_GROUPS_PER_CHUNK == 0 else G
    n = G // gc
    eye = jnp.eye(gc, dtype=F32)

    def bu(bb):
        return jnp.einsum('ngcp,gh->ngchp', bb.reshape(n, gc, C, P), eye).reshape(n, gc * C, gc * P)

    def cy(c):
        return jnp.einsum('ngcp,gh->ngphc', c.reshape(n, gc, C, P), eye).reshape(n, gc * P, gc * C)

    wbu = jnp.concatenate([bu(bb_re_t), bu(bb_im_t)], axis=-1).astype(BF16)
    return wbu, cy(c_re).astype(BF16), cy(c_im).astype(BF16)


def _layer_tokens(x, h, wl, n_seq, q_scale, kv_all, branch_fn, ffn_fn, emit_h):
    q, k, v, kb, vb, u, qm, lf = inproj(h, wl['w_qkvf'], wl['w_um'], wl['layer'], wl['b_f'], wl['wa'], wl['wb'],
                                        wl['wm'], wl['nh'], n_seq, q_scale, kv_all)
    o_a, o_b, o_m, extra = branch_fn(q, kb, vb, lf, u, qm)
    merged = merge(h, o_a, o_b, o_m, wl['w_gate'], wl['layer'], wl['p_a'], wl['p_b'], wl['p_m'])
    x, h2 = proj_residual(merged, wl['w_out'], x, wl['g_post_mix'], wl['g_pre_ffn'])
    act, conv_new = ffn_fn(h2)
    x, h_next = proj_residual(act, wl['w_down'], x, wl['g_post_ffn'], wl['g_next'], emit_h=emit_h)
    return x, h_next, k, v, lf, conv_new, extra


def kernel(x_prompt, x_sample, cache_k, cache_v, cache_logf, cache_mem_k, cache_mem_v, state_ssm_re, state_ssm_im, state_conv, page_table, mem_prompt, w_in, b_f, a_re, a_im, log_dt, b_re, b_im, c_re, c_im, d_skip, w_glu, g_mem, w_mkv, p_a, p_b, p_m, w_out, g_pre_mix, g_post_mix, g_pre_ffn, g_post_ffn, w_up, conv_w, conv_b, w_down):
    Bp, L, D = x_prompt.shape
    Bd, T, _ = x_sample.shape
    depth = w_in.shape[0]
    wa = p_a.shape[1]
    wb = p_b.shape[1]
    wm = p_m.shape[1]
    nh = b_f.shape[1]
    nhm = wm // HEAD_DIM
    G, P = a_re.shape[1:]
    C = b_re.shape[3]
    gp = G * P
    n_mem = mem_prompt.shape[1]
    n_pool, page = cache_k.shape[1:3]
    F2 = w_up.shape[2]

    o_u = 3 * wa + nh
    o_g = o_u + wb + wm
    w_qkvf = jnp.pad(w_in[:, :, :o_u], ((0, 0), (0, 0), (0, LANES - nh))).astype(BF16)
    w_um = w_in[:, :, o_u:o_g].astype(BF16)
    w_gate = w_in[:, :, o_g:].astype(BF16)
    b_f_pad = jnp.pad(b_f, ((0, 0), (0, LANES - nh))).reshape(depth, 1, LANES)
    ab_re, ab_im, bb_re_t, bb_im_t = s5_discretize(a_re, a_im, log_dt, b_re.transpose(0, 1, 3, 2),
                                                   b_im.transpose(0, 1, 3, 2))
    bf = lambda a: a.astype(BF16)
    w_mkv_b, p_a_b, p_b_b, p_m_b, w_out_b, w_down_b, w_glu_b = map(
        bf, (w_mkv, p_a, p_b, p_m, w_out, w_down, w_glu))

    mk, mv, mkb, mvb = mem_kv(mem_prompt.reshape(Bp * n_mem, D), g_mem.reshape(depth, 1, D), w_mkv_b)

    cache_k2 = cache_k.transpose(0, 1, 3, 2, 4)
    cache_v2 = cache_v.transpose(0, 1, 3, 2, 4)
    cache_lf_t = jnp.pad(cache_logf.transpose(0, 1, 3, 2), ((0, 0), (0, 0), (0, HEADS_PAD - nh), (0, 0)))
    cache_mk_b = bf(cache_mem_k).reshape(depth, Bd, n_mem, wm)
    cache_mv_b = bf(cache_mem_v).reshape(depth, Bd, n_mem, wm)

    xp = x_prompt.reshape(Bp * L, D)
    xs = x_sample.reshape(Bd * T, D)
    hp = rmsnorm_bf16(xp, g_pre_mix[0])
    hs = rmsnorm_bf16(xs, g_pre_mix[0])
    zeros_p = jnp.zeros((Bp, gp), F32)

    outs = [[] for _ in range(14)]
    kv_p = kv_s = None
    for l in range(depth):
        wbu, wc_re, wc_im = _s5_weights(bb_re_t[l], bb_im_t[l], c_re[l], c_im[l])
        wl = dict(w_qkvf=w_qkvf, w_um=w_um, w_gate=w_gate, layer=l, b_f=b_f_pad[l], wa=wa, wb=wb, wm=wm, nh=nh,
                  p_a=p_a_b[l], p_b=p_b_b[l], p_m=p_m_b[l], w_out=w_out_b[l], w_down=w_down_b[l],
                  g_post_mix=g_post_mix[l], g_pre_ffn=g_pre_ffn[l], g_post_ffn=g_post_ffn[l],
                  g_next=g_pre_mix[(l + 1) % depth])
        s5_args = (ab_re[l].reshape(1, gp), ab_im[l].reshape(1, gp), wbu, wc_re, wc_im,
                   d_skip[l].reshape(1, wb), w_glu_b[l])
        emit_h = l + 1 < depth

        def prompt_branches(q, kb, vb, lf, u, qm):
            lf_t = jnp.pad(lf.reshape(Bp, L, nh).transpose(0, 2, 1), ((0, 0), (0, HEADS_PAD - nh), (0, 0)))
            qa, ka = fox_bias_features(lf_t, nh)
            o_a = fox_prompt(q, qa, kb, ka, vb).reshape(Bp * L, wa)
            u_tb = u.reshape(Bp, L, wb).transpose(1, 0, 2).reshape(L * Bp, wb)
            o_b, hre, him = s5_branch(u_tb, zeros_p, zeros_p, *s5_args, nb=Bp)
            o_b = o_b.reshape(L, Bp, wb).transpose(1, 0, 2).reshape(Bp * L, wb)
            o_m = mem_attn(qm.reshape(Bp, L, wm), mkb[l].reshape(Bp, n_mem, wm),
                           mvb[l].reshape(Bp, n_mem, wm)).reshape(Bp * L, wm)
            return o_a, o_b, o_m, (hre, him)

        def prompt_ffn(h2):
            act, sa, sb = ffn_up(h2, w_up, l, conv_w[l], conv_b[l].reshape(1, F2), Bp)
            return act, jnp.concatenate([sa, sb], axis=-1)

        xp, hp, k, v, lf, cv, (hre, him) = _layer_tokens(xp, hp, wl, Bp, HEAD_DIM ** -0.5 * LOG2E, kv_p,
                                                         prompt_branches, prompt_ffn, emit_h)
        kv_p = (k, v)
        for i, a in zip((2, 3, 4, 5, 6, 7), (
                lf.reshape(Bp, L, nh),
                mk[l].reshape(Bp, n_mem, nhm, HEAD_DIM), mv[l].reshape(Bp, n_mem, nhm, HEAD_DIM),
                hre.reshape(Bp, G, P), him.reshape(Bp, G, P), cv)):
            outs[i].append(a)

        def sample_branches(q, kb, vb, lf, u, qm):
            tp = -(-T // SUBLANES) * SUBLANES
            padrows = lambda a, n: jnp.pad(a, ((0, 0), (0, 0), (0, n - T), (0, 0)))
            lx = jnp.pad(lf.reshape(Bd, T, nh).transpose(0, 2, 1), ((0, 0), (0, 0), (0, LANES - T)))
            lx = jnp.broadcast_to(lx[:, :, None], (Bd, nh, tp, LANES)).reshape(Bd, nh * tp, LANES)
            o = fox_decode(l, page_table, padrows(q, tp), lx, padrows(kb, page), padrows(vb, page),
                           cache_k2, cache_v2, cache_lf_t, T)
            o_a = o.reshape(Bd, nh, tp, HEAD_DIM)[:, :, :T].transpose(0, 2, 1, 3).reshape(Bd * T, wa).astype(BF16)
            u_tb = u.reshape(Bd, T, wb).transpose(1, 0, 2).reshape(T * Bd, wb)
            o_b, hre, him = s5_branch(u_tb, state_ssm_re[l].reshape(Bd, gp), state_ssm_im[l].reshape(Bd, gp),
                                      *s5_args, nb=Bd)
            o_b = o_b.reshape(T, Bd, wb).transpose(1, 0, 2).reshape(Bd * T, wb)
            o_m = mem_attn(qm.reshape(Bd, T, wm), cache_mk_b[l], cache_mv_b[l]).reshape(Bd * T, wm)
            return o_a, o_b, o_m, (hre, him)

        def sample_ffn(h2):
            h_tb = h2.reshape(Bd, T, D).transpose(1, 0, 2).reshape(T * Bd, D)
            st_tb = state_conv[l].transpose(1, 0, 2).reshape(2 * Bd, F2)
            act, sa, sb = ffn_up_step(h_tb, w_up, l, conv_w[l], conv_b[l].reshape(1, F2), st_tb, Bd)
            act = act.reshape(T, Bd, F2 // 2).transpose(1, 0, 2).reshape(Bd * T, F2 // 2)
            cv = jnp.concatenate([sa, sb], axis=-1).reshape(2, Bd, F2).transpose(1, 0, 2)
            return act, cv

        xs, hs, k, v, lf, cv, (hre, him) = _layer_tokens(xs, hs, wl, Bd, HEAD_DIM ** -0.5, kv_s,
                                                         sample_branches, sample_ffn, emit_h)
        kv_s = (k, v)
        for i, a in zip((10, 11, 12, 13), (
                lf.reshape(Bd, T, nh),
                hre.reshape(Bd, G, P), him.reshape(Bd, G, P), cv)):
            outs[i].append(a)

    res = [jnp.stack(o) if o else None for o in outs]
    for i, a in zip((0, 1, 8, 9), kv_p + kv_s):
        res[i] = a.transpose(0, 1, 3, 2, 4)
    return (xp.reshape(Bp, L, D), xs.reshape(Bd, T, D)) + tuple(res)
```

```python
import functools
import math

import jax
import jax.numpy as jnp
from jax import lax
from jax.experimental import pallas as pl
from jax.experimental.pallas import tpu as pltpu

F32 = jnp.float32
BF16 = jnp.bfloat16

RMS_EPS = 1e-6
HEAD_DIM = 128
GROUP_CH = 16
LANES = 128
SUBLANES = 8
MXU_DIM = 256
VMEM_LIMIT = 56 * 1024 * 1024
S5_GROUPS_PER_CHUNK = MXU_DIM // GROUP_CH
HEADS_PAD = SUBLANES
LOG2E = 1.4426950408889634
DECODE_PAGES_PER_STEP = 16


def _tile(n, pref):
    return pref if n % pref == 0 else n


def _params(sem, vmem=VMEM_LIMIT):
    return pltpu.CompilerParams(dimension_semantics=sem, vmem_limit_bytes=vmem)


def _resident(shape, index_map):
    return pl.BlockSpec(shape, index_map, pipeline_mode=pl.Buffered(1))


def _dot(a, b):
    return jnp.dot(a, b, preferred_element_type=F32)


def _dot_nt(a, b):
    return lax.dot_general(a, b, (((1,), (1,)), ((), ())), preferred_element_type=F32)


def _rms(x, g):
    return x * lax.rsqrt(jnp.mean(x * x, axis=-1, keepdims=True) + RMS_EPS) * g


def _sigmoid(x):
    return 1.0 / (1.0 + jnp.exp(-x))


def _log_sigmoid(x):
    return jnp.minimum(x, 0.0) - jnp.log1p(jnp.exp(-jnp.abs(x)))


def _top_half(x):
    bits = lax.bitcast_convert_type(x, jnp.uint32) & jnp.uint32(0xFFFF0000)
    return lax.bitcast_convert_type(bits, F32)


def _gelu_tanh(x):
    c = math.sqrt(2.0 / math.pi)
    return 0.5 * x * (1.0 + jnp.tanh(c * (x + 0.044715 * (x * x * x))))


def _rmsnorm_kernel(x_ref, g_ref, o_ref):
    o_ref[...] = _rms(x_ref[...], g_ref[...]).astype(o_ref.dtype)


def rmsnorm_bf16(x, g):
    M, D = x.shape
    bm = _tile(M, 512)
    return pl.pallas_call(
        _rmsnorm_kernel, grid=(M // bm,),
        in_specs=[pl.BlockSpec((bm, D), lambda i: (i, 0)), pl.BlockSpec((1, D), lambda i: (0, 0))],
        out_specs=pl.BlockSpec((bm, D), lambda i: (i, 0)),
        out_shape=jax.ShapeDtypeStruct((M, D), BF16),
        compiler_params=_params(("parallel",)), name="rmsnorm",
    )(x, g.reshape(1, D))


def _inproj_kernel(h_ref, wqkvf_ref, wum_ref, bf_ref, q_ref, k_ref, v_ref, kb_ref, vb_ref, u_ref, qm_ref, lf_ref,
                   *, wa, wb, wm, nh, q_scale):
    h = h_ref[...]
    scale = HEAD_DIM ** -0.5
    n_seq, _, seq_rows, _ = q_ref.shape

    def proj(a, b, w_ref=wqkvf_ref):
        return _dot(h, w_ref[:, a:b])

    def put_heads(ref, val):
        for s in range(n_seq):
            for hd in range(nh):
                ref[s, hd] = val[s * seq_rows:(s + 1) * seq_rows, hd * HEAD_DIM:(hd + 1) * HEAD_DIM]

    put_heads(q_ref, (proj(0, wa) * q_scale).astype(BF16))
    k = proj(wa, 2 * wa)
    put_heads(k_ref, k)
    put_heads(kb_ref, k.astype(BF16))
    v = proj(2 * wa, 3 * wa)
    put_heads(v_ref, v)
    put_heads(vb_ref, v.astype(BF16))
    o = 3 * wa
    fl = proj(o, o + LANES) + bf_ref[...]
    lf_ref[...] = _log_sigmoid(fl)[:, :nh]
    u_ref[...] = proj(0, wb, wum_ref)
    qm_ref[...] = (proj(wb, wb + wm, wum_ref) * scale).astype(BF16)


def _inproj_fill_kernel(h_ref, wqkvf_ref, wum_ref, bf_ref, kall_ref, vall_ref, *out_refs, **kw):
    del kall_ref, vall_ref
    _inproj_kernel(h_ref, wqkvf_ref, wum_ref, bf_ref, *out_refs, **kw)


def inproj(h, w_qkvf, w_um, layer, bf, wa, wb, wm, nh, n_seq, q_scale, kv_all):
    M, D = h.shape
    depth = w_qkvf.shape[0]
    L = M // n_seq
    bm = _tile(L, 512) if L >= SUBLANES else M
    tiles_per_seq = max(L // bm, 1)
    seqs_per_tile = max(bm // L, 1)
    rows_per_seq = min(bm, L)
    row = lambda n: pl.BlockSpec((bm, n), lambda i: (i, 0))
    heads = pl.BlockSpec((seqs_per_tile, nh, rows_per_seq, HEAD_DIM),
                         lambda i: (i // tiles_per_seq, 0, i % tiles_per_seq, 0))
    slab = pl.BlockSpec((None, seqs_per_tile, nh, rows_per_seq, HEAD_DIM),
                        lambda i: (layer, i // tiles_per_seq, 0, i % tiles_per_seq, 0))
    hshape = lambda dt: jax.ShapeDtypeStruct((n_seq, nh, L, HEAD_DIM), dt)
    all_shape = jax.ShapeDtypeStruct((depth, n_seq, nh, L, HEAD_DIM), F32)
    in_specs = [row(D), _resident((None, D, w_qkvf.shape[2]), lambda i: (layer, 0, 0)),
                _resident((None, D, w_um.shape[2]), lambda i: (layer, 0, 0)),
                pl.BlockSpec((1, LANES), lambda i: (0, 0))]
    kw = dict(wa=wa, wb=wb, wm=wm, nh=nh, q_scale=q_scale)
    if kv_all is None:
        body, extra, aliases = functools.partial(_inproj_kernel, **kw), (), {}
    else:
        body, extra, aliases = functools.partial(_inproj_fill_kernel, **kw), tuple(kv_all), {4: 1, 5: 2}
        in_specs += [pl.BlockSpec(memory_space=pl.ANY)] * 2
    return pl.pallas_call(
        body, grid=(M // bm,), in_specs=in_specs,
        out_specs=[heads, slab, slab, heads, heads, row(wb), row(wm), row(nh)],
        out_shape=[hshape(BF16), all_shape, all_shape, hshape(BF16), hshape(BF16),
                   jax.ShapeDtypeStruct((M, wb), F32), jax.ShapeDtypeStruct((M, wm), BF16),
                   jax.ShapeDtypeStruct((M, nh), F32)],
        input_output_aliases=aliases,
        compiler_params=_params(("parallel",)), name="inproj",
    )(h, w_qkvf, w_um, bf, *extra)


def _fox_bias_kernel(x_ref, qa_ref, ka_ref):
    x = x_ref[0]
    L = x.shape[-1]
    nh = qa_ref.shape[1]
    lane = lax.broadcasted_iota(jnp.int32, x.shape, 1)
    s = 1
    while s < L:
        x = x + jnp.where(lane >= s, pltpu.roll(x, s, axis=1), 0.0)
        s *= 2
    t = jnp.transpose(jnp.concatenate([x * LOG2E, jnp.zeros((LANES - x.shape[0], L), F32)], axis=0))

    col = lax.broadcasted_iota(jnp.int32, t.shape, 1)
    for h in range(nh):
        c = jnp.broadcast_to(t[:, h:h + 1], t.shape)
        hi = _top_half(c)
        mid = _top_half(c - hi)
        lo = c - hi - mid
        ones = lambda a, b: jnp.where((col >= a) & (col < b), 1.0, 0.0)
        qa = jnp.where(col == 0, hi, jnp.where(col == 1, mid, jnp.where(col == 2, lo, ones(3, 6))))
        ka = jnp.where(col == 3, -hi, jnp.where(col == 4, -mid, jnp.where(col == 5, -lo, ones(0, 3))))
        qa_ref[0, h] = qa.astype(BF16)
        ka_ref[0, h] = ka.astype(BF16)


def fox_bias_features(logf, nh):
    B, HP, L = logf.shape
    out = pl.BlockSpec((1, nh, L, HEAD_DIM), lambda b: (b, 0, 0, 0))
    return pl.pallas_call(
        _fox_bias_kernel, grid=(B,),
        in_specs=[pl.BlockSpec((1, HP, L), lambda b: (b, 0, 0))],
        out_specs=[out, out],
        out_shape=[jax.ShapeDtypeStruct((B, nh, L, HEAD_DIM), BF16)] * 2,
        compiler_params=_params(("parallel",)), name="fox_bias",
    )(logf)


def _fox_kernel(q_ref, qa_ref, k_ref, ka_ref, v_ref, o_ref, m_sc, l_sc, acc_sc):
    qi = pl.program_id(1)
    ki = pl.program_id(2)
    nh = q_ref.shape[0]

    @pl.when(ki == 0)
    def _():
        m_sc[...] = jnp.full(m_sc.shape, -jnp.inf, F32)
        l_sc[...] = jnp.zeros(l_sc.shape, F32)
        acc_sc[...] = jnp.zeros(acc_sc.shape, F32)

    def update(masked):
        heads = range(nh)
        s = [_dot_nt(jnp.concatenate([q_ref[h], qa_ref[h]], axis=1),
                     jnp.concatenate([k_ref[h], ka_ref[h]], axis=1)) for h in heads]
        if masked:
            row = lax.broadcasted_iota(jnp.int32, s[0].shape, 0)
            col = lax.broadcasted_iota(jnp.int32, s[0].shape, 1)
            s = [jnp.where(col <= row, sh, -jnp.inf) for sh in s]
        m_prev = [m_sc[h] for h in heads]
        m_new = [jnp.maximum(m_prev[h], jnp.max(s[h], axis=-1, keepdims=True)) for h in heads]
        alpha = [jnp.exp2(m_prev[h] - m_new[h]) for h in heads]
        p = [jnp.exp2(s[h] - m_new[h]) for h in heads]
        for h in heads:
            l_sc[h] = alpha[h] * l_sc[h] + jnp.sum(p[h], axis=-1, keepdims=True)
            m_sc[h] = m_new[h]
        pv = [_dot(p[h].astype(BF16), v_ref[h]) for h in heads]
        for h in heads:
            acc_sc[h] = alpha[h] * acc_sc[h] + pv[h]

    @pl.when(ki < qi)
    def _():
        update(False)

    @pl.when(ki == qi)
    def _():
        update(True)
        for h in range(nh):
            o_ref[:, h * HEAD_DIM:(h + 1) * HEAD_DIM] = (acc_sc[h] / l_sc[h]).astype(o_ref.dtype)


def fox_prompt(q, qa, k, ka, v):
    B, nh, L, _ = q.shape
    bq = _tile(L, 512)
    nq = L // bq
    qspec = pl.BlockSpec((None, nh, bq, HEAD_DIM), lambda b, qi, ki: (b, 0, qi, 0))
    kspec = pl.BlockSpec((None, nh, bq, HEAD_DIM), lambda b, qi, ki: (b, 0, jnp.minimum(ki, qi), 0))
    return pl.pallas_call(
        _fox_kernel, grid=(B, nq, nq),
        in_specs=[qspec, qspec, kspec, kspec, kspec],
        out_specs=pl.BlockSpec((None, bq, nh * HEAD_DIM), lambda b, qi, ki: (b, qi, 0)),
        out_shape=jax.ShapeDtypeStruct((B, L, nh * HEAD_DIM), BF16),
        scratch_shapes=[pltpu.VMEM((nh, bq, 1), F32), pltpu.VMEM((nh, bq, 1), F32),
                        pltpu.VMEM((nh, bq, HEAD_DIM), F32)],
        compiler_params=_params(("parallel", "parallel", "arbitrary")), name="fox_prompt",
    )(q, qa, k, ka, v)


def _fox_decode_kernel(pt_ref, q_ref, lx_ref, kn_ref, vn_ref, *refs, pps, n_new):
    kp = refs[:pps]
    vp = refs[pps:2 * pps]
    lp = refs[2 * pps:3 * pps]
    o_ref, m_sc, l_sc, acc_sc, suf_sc = refs[3 * pps:]
    del pt_ref
    step = pl.program_id(1)
    nh, tp, _ = q_ref.shape
    page = kn_ref.shape[1]

    @pl.when(step == 0)
    def _():
        m_sc[...] = jnp.full(m_sc.shape, -jnp.inf, F32)
        l_sc[...] = jnp.zeros(l_sc.shape, F32)
        acc_sc[...] = jnp.zeros(acc_sc.shape, F32)
        suf_sc[...] = jnp.zeros(suf_sc.shape, F32)

    ck = lx_ref[...]
    col = lax.broadcasted_iota(jnp.int32, ck.shape, 1)
    row_t = lax.broadcasted_iota(jnp.int32, ck.shape, 0) % tp
    s = 1
    while s < n_new:
        ck = ck + jnp.where(col >= s, pltpu.roll(ck, s, axis=1), 0.0)
        s *= 2
    cq = jnp.sum(jnp.where(col == row_t, ck, 0.0), axis=-1, keepdims=True)

    def update(sc, v_tiles):
        m_prev = m_sc[...]
        m_new = jnp.maximum(m_prev, jnp.max(sc, axis=-1, keepdims=True))
        alpha = jnp.exp(m_prev - m_new)
        p = jnp.exp(sc - m_new)
        l_sc[...] = alpha * l_sc[...] + jnp.sum(p, axis=-1, keepdims=True)
        pv = []
        for h in range(nh):
            ph = p[h * tp:(h + 1) * tp].astype(BF16)
            acc = _dot(ph[:, :page], v_tiles[0](h))
            for i in range(1, len(v_tiles)):
                acc += _dot(ph[:, i * page:(i + 1) * page], v_tiles[i](h))
            pv.append(acc)
        acc_sc[...] = alpha * acc_sc[...] + jnp.concatenate(pv, axis=0)
        m_sc[...] = m_new

    lane = lax.broadcasted_iota(jnp.int32, (HEADS_PAD, LANES), 1)
    after = suf_sc[...]
    biases = []
    for i in range(pps):
        lf = lp[i][...]
        inc = lf
        s = 1
        while s < LANES:
            inc = inc + jnp.where(lane < LANES - s, pltpu.roll(inc, LANES - s, axis=1), 0.0)
            s *= 2
        biases.append(inc - lf + after)
        after = after + jnp.broadcast_to(inc[:, 0:1], after.shape)
    suf_sc[...] = after
    bias = jnp.concatenate(biases, axis=1)

    rows = []
    for h in range(nh):
        qh = q_ref[h]
        sh = jnp.concatenate([_dot_nt(qh, kp[i][h].astype(BF16)) for i in range(pps)], axis=1)
        rows.append(sh + bias[h:h + 1])
    update(jnp.concatenate(rows, axis=0) + cq,
           [lambda h, i=i: vp[i][h].astype(BF16) for i in range(pps)])

    @pl.when(step == pl.num_programs(1) - 1)
    def _():
        sc = jnp.concatenate([_dot_nt(q_ref[h], kn_ref[h]) for h in range(nh)], axis=0) + cq - ck
        sc = jnp.where((col <= row_t) & (col < n_new), sc, -jnp.inf)
        update(sc, [lambda h: vn_ref[h]])
        o_ref[...] = acc_sc[...] / l_sc[...]


def fox_decode(layer, page_table, q, lx, kn, vn, cache_k, cache_v, cache_lf, n_new):
    Bd, nh, tp, _ = q.shape
    n_pages = page_table.shape[1]
    page = cache_k.shape[3]
    pps = DECODE_PAGES_PER_STEP if n_pages % DECODE_PAGES_PER_STEP == 0 else 1
    steps = n_pages // pps
    R = nh * tp

    def page_spec(i, shape):
        zeros = (0,) * len(shape)
        return pl.BlockSpec((None, None) + shape,
                            lambda b, s, pt: (layer, pt[b, n_pages - 1 - (s * pps + i)]) + zeros)

    def per_b(shape):
        zeros = (0,) * len(shape)
        return pl.BlockSpec((None,) + shape, lambda b, s, pt: (b,) + zeros)

    in_specs = ([per_b((nh, tp, HEAD_DIM)), per_b((R, LANES)), per_b((nh, page, HEAD_DIM)),
                 per_b((nh, page, HEAD_DIM))]
                + [page_spec(i, (nh, page, HEAD_DIM)) for i in range(pps)]
                + [page_spec(i, (nh, page, HEAD_DIM)) for i in range(pps)]
                + [page_spec(i, (HEADS_PAD, page)) for i in range(pps)])
    grid_spec = pltpu.PrefetchScalarGridSpec(
        num_scalar_prefetch=1, grid=(Bd, steps), in_specs=in_specs,
        out_specs=per_b((R, HEAD_DIM)),
        scratch_shapes=[pltpu.VMEM((R, 1), F32), pltpu.VMEM((R, 1), F32), pltpu.VMEM((R, HEAD_DIM), F32),
                        pltpu.VMEM((HEADS_PAD, LANES), F32)])
    return pl.pallas_call(
        functools.partial(_fox_decode_kernel, pps=pps, n_new=n_new), grid_spec=grid_spec,
        out_shape=jax.ShapeDtypeStruct((Bd, R, HEAD_DIM), F32),
        compiler_params=_params(("parallel", "arbitrary")), name="fox_decode",
    )(page_table, q, lx, kn, vn, *([cache_k] * pps), *([cache_v] * pps), *([cache_lf] * pps))


def _s5_disc_kernel(are_ref, aim_ref, ldt_ref, bre_ref, bim_ref, abre_ref, abim_ref, bbre_ref, bbim_ref):
    a_re = are_ref[0]
    a_im = aim_ref[0]
    dt = jnp.exp(ldt_ref[0])
    mag = jnp.exp(a_re * dt)
    ab_re = mag * jnp.cos(a_im * dt)
    ab_im = mag * jnp.sin(a_im * dt)
    den = a_re * a_re + a_im * a_im
    n_re = ab_re - 1.0
    q_re = (n_re * a_re + ab_im * a_im) / den
    q_im = (ab_im * a_re - n_re * a_im) / den
    b_re = bre_ref[0]
    b_im = bim_ref[0]
    abre_ref[0] = ab_re
    abim_ref[0] = ab_im
    bbre_ref[0] = q_re * b_re - q_im * b_im
    bbim_ref[0] = q_re * b_im + q_im * b_re


def s5_discretize(a_re, a_im, log_dt, b_re_t, b_im_t):
    Dp, G, P = a_re.shape
    C = b_re_t.shape[2]
    gp = pl.BlockSpec((1, G, 1, P), lambda l: (l, 0, 0, 0))
    gcp = pl.BlockSpec((1, G, C, P), lambda l: (l, 0, 0, 0))
    return pl.pallas_call(
        _s5_disc_kernel, grid=(Dp,),
        in_specs=[gp, gp, pl.BlockSpec((1, G, 1, 1), lambda l: (l, 0, 0, 0)), gcp, gcp],
        out_specs=[gp, gp, gcp, gcp],
        out_shape=[jax.ShapeDtypeStruct((Dp, G, 1, P), F32)] * 2 + [jax.ShapeDtypeStruct((Dp, G, C, P), F32)] * 2,
        compiler_params=_params(("parallel",)), name="s5_discretize",
    )(a_re.reshape(Dp, G, 1, P), a_im.reshape(Dp, G, 1, P), log_dt.reshape(Dp, G, 1, 1), b_re_t, b_im_t)


def _s5_kernel(u_ref, h0re_ref, h0im_ref, are_ref, aim_ref, wbu_ref, wcre_ref, wcim_ref, d_ref, wglu_ref,
               o_ref, hre_ref, him_ref, bre_sc, bim_sc, *, nb, lane_chunk, batch_major):
    n = pl.program_id(0)
    rows = bre_sc.shape[0]
    tc = rows // nb
    gp = bre_sc.shape[1]
    kc = wbu_ref.shape[1]
    sc = wbu_ref.shape[2] // 2
    n_chunks = wbu_ref.shape[0]

    @pl.when(n == 0)
    def _():
        hre_ref[...] = h0re_ref[...]
        him_ref[...] = h0im_ref[...]

    if batch_major:
        r_i = lax.broadcasted_iota(jnp.int32, (rows, tc), 0)
        t_i = lax.broadcasted_iota(jnp.int32, (rows, tc), 1)
        spread, terms = [], []
        for b in range(nb):
            x = u_ref[b]
            hi = _top_half(x)
            mid = _top_half(x - hi)
            terms += [hi.astype(BF16), mid.astype(BF16), (x - hi - mid).astype(BF16)]
            spread += [(r_i == t_i * nb + b).astype(BF16)] * 3
        u = _dot(jnp.concatenate(spread, axis=1), jnp.concatenate(terms, axis=0))
    else:
        u = u_ref[...]
    ub = u.astype(BF16)
    for c in range(n_chunks):
        bu = _dot(ub[:, c * kc:(c + 1) * kc], wbu_ref[c])
        bre_sc[:, c * sc:(c + 1) * sc] = bu[:, :sc]
        bim_sc[:, c * sc:(c + 1) * sc] = bu[:, sc:]

    spt = SUBLANES // nb
    for c in range(gp // lane_chunk):
        cs = slice(c * lane_chunk, (c + 1) * lane_chunk)
        ar = jnp.broadcast_to(are_ref[:, cs], (SUBLANES, lane_chunk))
        ai = jnp.broadcast_to(aim_ref[:, cs], (SUBLANES, lane_chunk))
        band = lax.broadcasted_iota(jnp.int32, (SUBLANES, lane_chunk), 0) // nb

        def step(t, carry, cs=cs, ar=ar, ai=ai, band=band):
            hr, hi = carry
            r0 = pl.multiple_of(t * SUBLANES, SUBLANES)
            xr = bre_sc[pl.ds(r0, SUBLANES), cs]
            xi = bim_sc[pl.ds(r0, SUBLANES), cs]
            out_r = out_i = None
            for k in range(spt):
                hr, hi = ar * hr - ai * hi + xr, ar * hi + ai * hr + xi
                out_r = hr if k == 0 else jnp.where(band == k, hr, out_r)
                out_i = hi if k == 0 else jnp.where(band == k, hi, out_i)
                if spt > 1:
                    hr = pltpu.roll(hr, nb, axis=0)
                    hi = pltpu.roll(hi, nb, axis=0)
            bre_sc[pl.ds(r0, SUBLANES), cs] = out_r
            bim_sc[pl.ds(r0, SUBLANES), cs] = out_i
            return hr, hi

        init = lambda ref: jnp.concatenate([ref[:, cs]] * spt, axis=0)
        hr, hi = lax.fori_loop(0, rows // SUBLANES, step, (init(hre_ref), init(him_ref)))
        hre_ref[:, cs] = hr[:nb]
        him_ref[:, cs] = hi[:nb]

    ys = []
    for c in range(n_chunks):
        hr = bre_sc[:, c * sc:(c + 1) * sc].astype(BF16)
        hi = bim_sc[:, c * sc:(c + 1) * sc].astype(BF16)
        ys.append(_dot(hr, wcre_ref[c]) - _dot(hi, wcim_ref[c]))
    y = jnp.concatenate(ys, axis=1) + d_ref[...] * u
    z = _gelu_tanh(y)
    o = (z * _sigmoid(_dot(z.astype(BF16), wglu_ref[...]))).astype(BF16)
    if batch_major:
        i_i = lax.broadcasted_iota(jnp.int32, (rows, rows), 0)
        r_i = lax.broadcasted_iota(jnp.int32, (rows, rows), 1)
        gather = (r_i == (i_i % tc) * nb + i_i // tc).astype(BF16)
        ob = _dot(gather, o).astype(BF16)
        for b in range(nb):
            o_ref[b] = ob[b * tc:(b + 1) * tc]
    else:
        o_ref[...] = o


def s5_branch(u, h0_re, h0_im, ab_re, ab_im, wbu, wc_re, wc_im, d_skip, w_glu, nb):
    batch_major = u.ndim == 3
    L = u.shape[1] if batch_major else u.shape[0] // nb
    wb = u.shape[-1]
    gp = h0_re.shape[1]
    tc = _tile(L, 128)
    rows = tc * nb
    lane_chunk = _tile(gp, 768)
    full = lambda a: _resident(a.shape, lambda n: (0,) * a.ndim)
    st = pl.BlockSpec((nb, gp), lambda n: (0, 0))
    io = (pl.BlockSpec((nb, tc, wb), lambda n: (0, n, 0)) if batch_major
          else pl.BlockSpec((rows, wb), lambda n: (n, 0)))
    return pl.pallas_call(
        functools.partial(_s5_kernel, nb=nb, lane_chunk=lane_chunk, batch_major=batch_major), grid=(L // tc,),
        in_specs=[io, st, st, full(ab_re), full(ab_im),
                  full(wbu), full(wc_re), full(wc_im), full(d_skip), full(w_glu)],
        out_specs=[io, st, st],
        out_shape=[jax.ShapeDtypeStruct(u.shape, BF16), jax.ShapeDtypeStruct((nb, gp), F32),
                   jax.ShapeDtypeStruct((nb, gp), F32)],
        scratch_shapes=[pltpu.VMEM((rows, gp), F32), pltpu.VMEM((rows, gp), F32)],
        compiler_params=_params(("arbitrary",)), name="s5_branch",
    )(u, h0_re, h0_im, ab_re, ab_im, wbu, wc_re, wc_im, d_skip, w_glu)


def _mem_kv_kernel(x_ref, g_ref, w_ref, k_ref, v_ref, kb_ref, vb_ref):
    h = _rms(x_ref[...], g_ref[0]).astype(BF16)
    z = _dot(h, w_ref[0])
    wm = k_ref.shape[-1]
    k_ref[0] = z[:, :wm]
    v_ref[0] = z[:, wm:]
    kb_ref[0] = z[:, :wm].astype(BF16)
    vb_ref[0] = z[:, wm:].astype(BF16)


def mem_kv(mem, g_mem, w_mkv):
    R, D = mem.shape
    Dp, _, W2 = w_mkv.shape
    wm = W2 // 2
    bm = _tile(R, 512)
    o = pl.BlockSpec((1, bm, wm), lambda l, i: (l, i, 0))
    return pl.pallas_call(
        _mem_kv_kernel, grid=(Dp, R // bm),
        in_specs=[pl.BlockSpec((bm, D), lambda l, i: (i, 0)), pl.BlockSpec((1, 1, D), lambda l, i: (l, 0, 0)),
                  pl.BlockSpec((1, D, W2), lambda l, i: (l, 0, 0))],
        out_specs=[o, o, o, o],
        out_shape=[jax.ShapeDtypeStruct((Dp, R, wm), F32)] * 2 + [jax.ShapeDtypeStruct((Dp, R, wm), BF16)] * 2,
        compiler_params=_params(("parallel", "arbitrary")), name="mem_kv",
    )(mem, g_mem, w_mkv)


def _mem_attn_kernel(q_ref, k_ref, v_ref, o_ref):
    nh = q_ref.shape[-1] // HEAD_DIM
    for h in range(nh):
        hs = slice(h * HEAD_DIM, (h + 1) * HEAD_DIM)
        s = _dot_nt(q_ref[:, hs], k_ref[:, hs])
        p = jnp.exp(s - jnp.max(s, axis=-1, keepdims=True))
        p = p / jnp.sum(p, axis=-1, keepdims=True)
        o_ref[:, hs] = _dot(p.astype(BF16), v_ref[:, hs]).astype(o_ref.dtype)


def mem_attn(q, k, v):
    B, L, W = q.shape
    Nm = k.shape[1]
    bq = _tile(L, 512)
    qs = pl.BlockSpec((None, bq, W), lambda b, i: (b, i, 0))
    ks = pl.BlockSpec((None, Nm, W), lambda b, i: (b, 0, 0))
    return pl.pallas_call(
        _mem_attn_kernel, grid=(B, L // bq), in_specs=[qs, ks, ks], out_specs=qs,
        out_shape=jax.ShapeDtypeStruct((B, L, W), BF16),
        compiler_params=_params(("parallel", "parallel")), name="mem_attn",
    )(q, k, v)


def _merge_kernel(h_ref, oa_ref, ob_ref, om_ref, g0_ref, g1_ref, g2_ref, pa_ref, pb_ref, pm_ref, o_ref):
    h = h_ref[...]
    acc = _sigmoid(_dot(h, g0_ref[...])) * _dot(oa_ref[...], pa_ref[...])
    acc += _sigmoid(_dot(h, g1_ref[...])) * _dot(ob_ref[...], pb_ref[...])
    acc += _sigmoid(_dot(h, g2_ref[...])) * _dot(om_ref[...], pm_ref[...])
    o_ref[...] = acc.astype(o_ref.dtype)


def merge(h, oa, ob, om, w, layer, pa, pb, pm):
    M, D = h.shape
    bm = _tile(M, 1024)
    bn = _tile(D, 512)
    nj = D // bn
    rowspec = lambda a: pl.BlockSpec((bm, a.shape[1]), lambda j, i: (i, 0))
    colspec = lambda a: pl.BlockSpec((a.shape[0], bn), lambda j, i: (0, j))
    gate = lambda b: pl.BlockSpec((None, D, bn), lambda j, i: (layer, 0, b * nj + j))
    return pl.pallas_call(
        _merge_kernel, grid=(nj, M // bm),
        in_specs=[rowspec(h), rowspec(oa), rowspec(ob), rowspec(om), gate(0), gate(1), gate(2),
                  colspec(pa), colspec(pb), colspec(pm)],
        out_specs=pl.BlockSpec((bm, bn), lambda j, i: (i, j)),
        out_shape=jax.ShapeDtypeStruct((M, D), BF16),
        compiler_params=_params(("parallel", "arbitrary")), name="merge",
    )(h, oa, ob, om, w, w, w, pa, pb, pm)


def _proj_res_kernel(a_ref, w_ref, x_ref, gpost_ref, gnext_ref, xo_ref, *h_ref):
    f = _dot(a_ref[...], w_ref[...])
    x = x_ref[...] + _rms(f, gpost_ref[...])
    xo_ref[...] = x
    if h_ref:
        h_ref[0][...] = _rms(x, gnext_ref[...]).astype(BF16)


def proj_residual(a, w, x, g_post, g_next, emit_h=True):
    M, K = a.shape
    D = w.shape[1]
    bm = _tile(M, 256)
    row = lambda n: pl.BlockSpec((bm, n), lambda i: (i, 0))
    g = pl.BlockSpec((1, D), lambda i: (0, 0))
    out_specs = [row(D)] + ([row(D)] if emit_h else [])
    out_shape = [jax.ShapeDtypeStruct((M, D), F32)] + ([jax.ShapeDtypeStruct((M, D), BF16)] if emit_h else [])
    res = pl.pallas_call(
        _proj_res_kernel, grid=(M // bm,),
        in_specs=[row(K), _resident((K, D), lambda i: (0, 0)), row(D), g, g],
        out_specs=out_specs, out_shape=out_shape,
        compiler_params=_params(("parallel",)), name="proj_residual",
    )(a, w, x, g_post.reshape(1, D), g_next.reshape(1, D))
    return (res[0], res[1]) if emit_h else (res[0], None)


def _ffn_up_kernel(h_ref, wa_ref, wb_ref, cwa_ref, cwb_ref, cba_ref, cbb_ref, act_ref, sa_ref, sb_ref,
                   wa_sc, wb_sc, ca_sc, cb_sc, *, tiles_per_seq, sub):
    i = pl.program_id(1)
    bm = h_ref.shape[0]

    @pl.when(i == 0)
    def _():
        wa_sc[...] = wa_ref[...].astype(BF16)
        wb_sc[...] = wb_ref[...].astype(BF16)

    @pl.when(i % tiles_per_seq == 0)
    def _():
        ca_sc[...] = jnp.zeros(ca_sc.shape, F32)
        cb_sc[...] = jnp.zeros(cb_sc.shape, F32)

    def conv(u, prev, cw_ref, cb_ref):
        row = lax.broadcasted_iota(jnp.int32, u.shape, 0)
        u1 = jnp.where(row == 0, prev[SUBLANES - 1:SUBLANES], pltpu.roll(u, 1, axis=0))
        u2 = jnp.where(row == 0, prev[SUBLANES - 2:SUBLANES - 1],
                       jnp.where(row == 1, prev[SUBLANES - 1:SUBLANES], pltpu.roll(u, 2, axis=0)))
        return cb_ref[...] + cw_ref[2:3] * u + cw_ref[1:2] * u1 + cw_ref[0:1] * u2

    prev_a = ca_sc[...]
    prev_b = cb_sc[...]
    for r in range(bm // sub):
        rows = slice(r * sub, (r + 1) * sub)
        h = h_ref[rows, :]
        ua = _dot(h, wa_sc[...])
        ub = _dot(h, wb_sc[...])
        ya = conv(ua, prev_a, cwa_ref, cba_ref)
        yb = conv(ub, prev_b, cwb_ref, cbb_ref)
        act_ref[rows, :] = (_gelu_tanh(ya) * yb).astype(act_ref.dtype)
        prev_a = ua[sub - SUBLANES:]
        prev_b = ub[sub - SUBLANES:]
    ca_sc[...] = prev_a
    cb_sc[...] = prev_b
    sa_ref[...] = prev_a[SUBLANES - 2:]
    sb_ref[...] = prev_b[SUBLANES - 2:]


def ffn_up(h, w_up, layer, conv_w, conv_b, n_seq):
    M, D = h.shape
    F2 = w_up.shape[2]
    F = F2 // 2
    L = M // n_seq
    bm = _tile(L, 1024)
    bn = _tile(F, 512)
    sub = bm
    nj = F // bn
    tps = L // bm
    a_col = lambda r: pl.BlockSpec((r, bn), lambda j, i: (0, j))
    b_col = lambda r: pl.BlockSpec((r, bn), lambda j, i: (0, nj + j))
    wa = pl.BlockSpec((None, D, bn), lambda j, i: (layer, 0, j))
    wb = pl.BlockSpec((None, D, bn), lambda j, i: (layer, 0, nj + j))
    st = pl.BlockSpec((None, 2, bn), lambda j, i: (i // tps, 0, j))
    return pl.pallas_call(
        functools.partial(_ffn_up_kernel, tiles_per_seq=tps, sub=sub), grid=(nj, M // bm),
        in_specs=[pl.BlockSpec((bm, D), lambda j, i: (i, 0)), wa, wb, a_col(3), b_col(3), a_col(1), b_col(1)],
        out_specs=[pl.BlockSpec((bm, bn), lambda j, i: (i, j)), st, st],
        out_shape=[jax.ShapeDtypeStruct((M, F), BF16), jax.ShapeDtypeStruct((n_seq, 2, F), F32),
                   jax.ShapeDtypeStruct((n_seq, 2, F), F32)],
        scratch_shapes=[pltpu.VMEM((D, bn), BF16), pltpu.VMEM((D, bn), BF16),
                        pltpu.VMEM((SUBLANES, bn), F32), pltpu.VMEM((SUBLANES, bn), F32)],
        compiler_params=_params(("parallel", "arbitrary")), name="ffn_up",
    )(h, w_up, w_up, conv_w, conv_w, conv_b, conv_b)


def _ffn_up_step_kernel(h_ref, wa_ref, wb_ref, sta_ref, stb_ref, cwa_ref, cwb_ref, cba_ref, cbb_ref,
                        act_ref, sa_ref, sb_ref, *, n_seq):
    h = h_ref[...]
    T = h.shape[0] // n_seq

    def conv(w_ref, st_ref, cw_ref, cb_ref, s_ref):
        u = _dot(h, w_ref[...].astype(BF16))
        ext = jnp.concatenate([st_ref[...], u], axis=0)
        s_ref[...] = ext[T * n_seq:]
        return (cb_ref[...] + cw_ref[2:3] * u + cw_ref[1:2] * ext[n_seq:(T + 1) * n_seq]
                + cw_ref[0:1] * ext[:T * n_seq])

    ya = conv(wa_ref, sta_ref, cwa_ref, cba_ref, sa_ref)
    yb = conv(wb_ref, stb_ref, cwb_ref, cbb_ref, sb_ref)
    act_ref[...] = (_gelu_tanh(ya) * yb).astype(act_ref.dtype)


def ffn_up_step(h_tb, w_up, layer, conv_w, conv_b, state_tb, n_seq):
    M, D = h_tb.shape
    F2 = w_up.shape[2]
    F = F2 // 2
    bn = _tile(F, 512)
    nj = F // bn
    a_col = lambda r: pl.BlockSpec((r, bn), lambda j: (0, j))
    b_col = lambda r: pl.BlockSpec((r, bn), lambda j: (0, nj + j))
    wa = pl.BlockSpec((None, D, bn), lambda j: (layer, 0, j))
    wb = pl.BlockSpec((None, D, bn), lambda j: (layer, 0, nj + j))
    return pl.pallas_call(
        functools.partial(_ffn_up_step_kernel, n_seq=n_seq), grid=(nj,),
        in_specs=[pl.BlockSpec((M, D), lambda j: (0, 0)), wa, wb, a_col(2 * n_seq), b_col(2 * n_seq),
                  a_col(3), b_col(3), a_col(1), b_col(1)],
        out_specs=[a_col(M), a_col(2 * n_seq), a_col(2 * n_seq)],
        out_shape=[jax.ShapeDtypeStruct((M, F), BF16), jax.ShapeDtypeStruct((2 * n_seq, F), F32),
                   jax.ShapeDtypeStruct((2 * n_seq, F), F32)],
        compiler_params=_params(("parallel",)), name="ffn_up_step",
    )(h_tb, w_up, w_up, state_tb, state_tb, conv_w, conv_w, conv_b, conv_b)


def _s5_weights(bb_re_t, bb_im_t, c_re, c_im):
    G, C, P = bb_re_t.shape
    gc = S5_GROUPS_PER_CHUNK if G % S5_GROUPS_PER_CHUNK == 0 else G
    n = G // gc
    eye = jnp.eye(gc, dtype=F32)

    def bu(bb):
        return jnp.einsum('ngcp,gh->ngchp', bb.reshape(n, gc, C, P), eye).reshape(n, gc * C, gc * P)

    def cy(c):
        return jnp.einsum('ngcp,gh->ngphc', c.reshape(n, gc, C, P), eye).reshape(n, gc * P, gc * C)

    wbu = jnp.concatenate([bu(bb_re_t), bu(bb_im_t)], axis=-1).astype(BF16)
    return wbu, cy(c_re).astype(BF16), cy(c_im).astype(BF16)


def _layer_tokens(x, h, wl, n_seq, q_scale, kv_all, branch_fn, ffn_fn, emit_h):
    q, k, v, kb, vb, u, qm, lf = inproj(h, wl['w_qkvf'], wl['w_um'], wl['layer'], wl['b_f'], wl['wa'], wl['wb'],
                                        wl['wm'], wl['nh'], n_seq, q_scale, kv_all)
    o_a, o_b, o_m, extra = branch_fn(q, kb, vb, lf, u, qm)
    merged = merge(h, o_a, o_b, o_m, wl['w_gate'], wl['layer'], wl['p_a'], wl['p_b'], wl['p_m'])
    x, h2 = proj_residual(merged, wl['w_out'], x, wl['g_post_mix'], wl['g_pre_ffn'])
    act, conv_new = ffn_fn(h2)
    x, h_next = proj_residual(act, wl['w_down'], x, wl['g_post_ffn'], wl['g_next'], emit_h=emit_h)
    return x, h_next, k, v, lf, conv_new, extra


def kernel(x_prompt, x_sample, cache_k, cache_v, cache_logf, cache_mem_k, cache_mem_v, state_ssm_re, state_ssm_im, state_conv, page_table, mem_prompt, w_in, b_f, a_re, a_im, log_dt, b_re, b_im, c_re, c_im, d_skip, w_glu, g_mem, w_mkv, p_a, p_b, p_m, w_out, g_pre_mix, g_post_mix, g_pre_ffn, g_post_ffn, w_up, conv_w, conv_b, w_down):
    Bp, L, D = x_prompt.shape
    Bd, T, _ = x_sample.shape
    depth = w_in.shape[0]
    wa = p_a.shape[1]
    wb = p_b.shape[1]
    wm = p_m.shape[1]
    nh = b_f.shape[1]
    nhm = wm // HEAD_DIM
    G, P = a_re.shape[1:]
    C = b_re.shape[3]
    gp = G * P
    n_mem = mem_prompt.shape[1]
    n_pool, page = cache_k.shape[1:3]
    F2 = w_up.shape[2]

    o_u = 3 * wa + nh
    o_g = o_u + wb + wm
    w_qkvf = jnp.pad(w_in[:, :, :o_u], ((0, 0), (0, 0), (0, LANES - nh))).astype(BF16)
    w_um = w_in[:, :, o_u:o_g].astype(BF16)
    w_gate = w_in[:, :, o_g:].astype(BF16)
    b_f_pad = jnp.pad(b_f, ((0, 0), (0, LANES - nh))).reshape(depth, 1, LANES)
    ab_re, ab_im, bb_re_t, bb_im_t = s5_discretize(a_re, a_im, log_dt, b_re.transpose(0, 1, 3, 2),
                                                   b_im.transpose(0, 1, 3, 2))
    bf = lambda a: a.astype(BF16)
    w_mkv_b, p_a_b, p_b_b, p_m_b, w_out_b, w_down_b, w_glu_b = map(
        bf, (w_mkv, p_a, p_b, p_m, w_out, w_down, w_glu))

    mk, mv, mkb, mvb = mem_kv(mem_prompt.reshape(Bp * n_mem, D), g_mem.reshape(depth, 1, D), w_mkv_b)

    cache_k2 = cache_k.transpose(0, 1, 3, 2, 4)
    cache_v2 = cache_v.transpose(0, 1, 3, 2, 4)
    cache_lf_t = jnp.pad(cache_logf.transpose(0, 1, 3, 2), ((0, 0), (0, 0), (0, HEADS_PAD - nh), (0, 0)))
    cache_mk_b = bf(cache_mem_k).reshape(depth, Bd, n_mem, wm)
    cache_mv_b = bf(cache_mem_v).reshape(depth, Bd, n_mem, wm)

    xp = x_prompt.reshape(Bp * L, D)
    xs = x_sample.reshape(Bd * T, D)
    hp = rmsnorm_bf16(xp, g_pre_mix[0])
    hs = rmsnorm_bf16(xs, g_pre_mix[0])
    zeros_p = jnp.zeros((Bp, gp), F32)

    outs = [[] for _ in range(14)]
    kv_p = kv_s = None
    for l in range(depth):
        wbu, wc_re, wc_im = _s5_weights(bb_re_t[l], bb_im_t[l], c_re[l], c_im[l])
        wl = dict(w_qkvf=w_qkvf, w_um=w_um, w_gate=w_gate, layer=l, b_f=b_f_pad[l], wa=wa, wb=wb, wm=wm, nh=nh,
                  p_a=p_a_b[l], p_b=p_b_b[l], p_m=p_m_b[l], w_out=w_out_b[l], w_down=w_down_b[l],
                  g_post_mix=g_post_mix[l], g_pre_ffn=g_pre_ffn[l], g_post_ffn=g_post_ffn[l],
                  g_next=g_pre_mix[(l + 1) % depth])
        s5_args = (ab_re[l].reshape(1, gp), ab_im[l].reshape(1, gp), wbu, wc_re, wc_im,
                   d_skip[l].reshape(1, wb), w_glu_b[l])
        emit_h = l + 1 < depth

        def prompt_branches(q, kb, vb, lf, u, qm):
            lf_t = jnp.pad(lf.reshape(Bp, L, nh).transpose(0, 2, 1), ((0, 0), (0, HEADS_PAD - nh), (0, 0)))
            qa, ka = fox_bias_features(lf_t, nh)
            o_a = fox_prompt(q, qa, kb, ka, vb).reshape(Bp * L, wa)
            o_b, hre, him = s5_branch(u.reshape(Bp, L, wb), zeros_p, zeros_p, *s5_args, nb=Bp)
            o_b = o_b.reshape(Bp * L, wb)
            o_m = mem_attn(qm.reshape(Bp, L, wm), mkb[l].reshape(Bp, n_mem, wm),
                           mvb[l].reshape(Bp, n_mem, wm)).reshape(Bp * L, wm)
            return o_a, o_b, o_m, (hre, him)

        def prompt_ffn(h2):
            act, sa, sb = ffn_up(h2, w_up, l, conv_w[l], conv_b[l].reshape(1, F2), Bp)
            return act, jnp.concatenate([sa, sb], axis=-1)

        xp, hp, k, v, lf, cv, (hre, him) = _layer_tokens(xp, hp, wl, Bp, HEAD_DIM ** -0.5 * LOG2E, kv_p,
                                                         prompt_branches, prompt_ffn, emit_h)
        kv_p = (k, v)
        for i, a in zip((2, 3, 4, 5, 6, 7), (
                lf.reshape(Bp, L, nh),
                mk[l].reshape(Bp, n_mem, nhm, HEAD_DIM), mv[l].reshape(Bp, n_mem, nhm, HEAD_DIM),
                hre.reshape(Bp, G, P), him.reshape(Bp, G, P), cv)):
            outs[i].append(a)

        def sample_branches(q, kb, vb, lf, u, qm):
            tp = -(-T // SUBLANES) * SUBLANES
            padrows = lambda a, n: jnp.pad(a, ((0, 0), (0, 0), (0, n - T), (0, 0)))
            lx = jnp.pad(lf.reshape(Bd, T, nh).transpose(0, 2, 1), ((0, 0), (0, 0), (0, LANES - T)))
            lx = jnp.broadcast_to(lx[:, :, None], (Bd, nh, tp, LANES)).reshape(Bd, nh * tp, LANES)
            o = fox_decode(l, page_table, padrows(q, tp), lx, padrows(kb, page), padrows(vb, page),
                           cache_k2, cache_v2, cache_lf_t, T)
            o_a = o.reshape(Bd, nh, tp, HEAD_DIM)[:, :, :T].transpose(0, 2, 1, 3).reshape(Bd * T, wa).astype(BF16)
            u_tb = u.reshape(Bd, T, wb).transpose(1, 0, 2).reshape(T * Bd, wb)
            o_b, hre, him = s5_branch(u_tb, state_ssm_re[l].reshape(Bd, gp), state_ssm_im[l].reshape(Bd, gp),
                                      *s5_args, nb=Bd)
            o_b = o_b.reshape(T, Bd, wb).transpose(1, 0, 2).reshape(Bd * T, wb)
            o_m = mem_attn(qm.reshape(Bd, T, wm), cache_mk_b[l], cache_mv_b[l]).reshape(Bd * T, wm)
            return o_a, o_b, o_m, (hre, him)

        def sample_ffn(h2):
            h_tb = h2.reshape(Bd, T, D).transpose(1, 0, 2).reshape(T * Bd, D)
            st_tb = state_conv[l].transpose(1, 0, 2).reshape(2 * Bd, F2)
            act, sa, sb = ffn_up_step(h_tb, w_up, l, conv_w[l], conv_b[l].reshape(1, F2), st_tb, Bd)
            act = act.reshape(T, Bd, F2 // 2).transpose(1, 0, 2).reshape(Bd * T, F2 // 2)
            cv = jnp.concatenate([sa, sb], axis=-1).reshape(2, Bd, F2).transpose(1, 0, 2)
            return act, cv

        xs, hs, k, v, lf, cv, (hre, him) = _layer_tokens(xs, hs, wl, Bd, HEAD_DIM ** -0.5, kv_s,
                                                         sample_branches, sample_ffn, emit_h)
        kv_s = (k, v)
        for i, a in zip((10, 11, 12, 13), (
                lf.reshape(Bd, T, nh),
                hre.reshape(Bd, G, P), him.reshape(Bd, G, P), cv)):
            outs[i].append(a)

    res = [jnp.stack(o) if o else None for o in outs]
    for i, a in zip((0, 1, 8, 9), kv_p + kv_s):
        res[i] = a.transpose(0, 1, 3, 2, 4)
    return (xp.reshape(Bp, L, D), xs.reshape(Bd, T, D)) + tuple(res)
```

```python
import functools
import math

import jax
import jax.numpy as jnp
from jax import lax
from jax.experimental import pallas as pl
from jax.experimental.pallas import tpu as pltpu

F32 = jnp.float32
BF16 = jnp.bfloat16

RMS_EPS = 1e-6
HEAD_DIM = 128
GROUP_CH = 16
LANES = 128
SUBLANES = 8
MXU_DIM = 256
VMEM_LIMIT = 56 * 1024 * 1024
S5_GROUPS_PER_CHUNK = MXU_DIM // GROUP_CH
HEADS_PAD = SUBLANES
LOG2E = 1.4426950408889634
DECODE_PAGES_PER_STEP = 16


def _tile(n, pref):
    return pref if n % pref == 0 else n


def _params(sem, vmem=VMEM_LIMIT):
    return pltpu.CompilerParams(dimension_semantics=sem, vmem_limit_bytes=vmem)


def _resident(shape, index_map):
    return pl.BlockSpec(shape, index_map, pipeline_mode=pl.Buffered(1))


def _dot(a, b):
    return jnp.dot(a, b, preferred_element_type=F32)


def _dot_nt(a, b):
    return lax.dot_general(a, b, (((1,), (1,)), ((), ())), preferred_element_type=F32)


def _rms(x, g):
    return x * lax.rsqrt(jnp.mean(x * x, axis=-1, keepdims=True) + RMS_EPS) * g


def _sigmoid(x):
    return 1.0 / (1.0 + jnp.exp(-x))


def _log_sigmoid(x):
    return jnp.minimum(x, 0.0) - jnp.log1p(jnp.exp(-jnp.abs(x)))


def _top_half(x):
    bits = lax.bitcast_convert_type(x, jnp.uint32) & jnp.uint32(0xFFFF0000)
    return lax.bitcast_convert_type(bits, F32)


def _gelu_tanh(x):
    c = math.sqrt(2.0 / math.pi)
    return 0.5 * x * (1.0 + jnp.tanh(c * (x + 0.044715 * (x * x * x))))


def _rmsnorm_kernel(x_ref, g_ref, o_ref):
    o_ref[...] = _rms(x_ref[...], g_ref[...]).astype(o_ref.dtype)


def rmsnorm_bf16(x, g):
    M, D = x.shape
    bm = _tile(M, 512)
    return pl.pallas_call(
        _rmsnorm_kernel, grid=(M // bm,),
        in_specs=[pl.BlockSpec((bm, D), lambda i: (i, 0)), pl.BlockSpec((1, D), lambda i: (0, 0))],
        out_specs=pl.BlockSpec((bm, D), lambda i: (i, 0)),
        out_shape=jax.ShapeDtypeStruct((M, D), BF16),
        compiler_params=_params(("parallel",)), name="rmsnorm",
    )(x, g.reshape(1, D))


def _inproj_kernel(h_ref, wqkvf_ref, wum_ref, bf_ref, q_ref, k_ref, v_ref, kb_ref, vb_ref, u_ref, qm_ref, lf_ref,
                   *, wa, wb, wm, nh, q_scale):
    h = h_ref[...]
    scale = HEAD_DIM ** -0.5
    n_seq, _, seq_rows, _ = q_ref.shape

    def proj(a, b, w_ref=wqkvf_ref):
        return _dot(h, w_ref[:, a:b])

    def put_heads(ref, val):
        for s in range(n_seq):
            for hd in range(nh):
                ref[s, hd] = val[s * seq_rows:(s + 1) * seq_rows, hd * HEAD_DIM:(hd + 1) * HEAD_DIM]

    put_heads(q_ref, (proj(0, wa) * q_scale).astype(BF16))
    k = proj(wa, 2 * wa)
    put_heads(k_ref, k)
    put_heads(kb_ref, k.astype(BF16))
    v = proj(2 * wa, 3 * wa)
    put_heads(v_ref, v)
    put_heads(vb_ref, v.astype(BF16))
    o = 3 * wa
    fl = proj(o, o + LANES) + bf_ref[...]
    lf_ref[...] = _log_sigmoid(fl)[:, :nh]
    u_ref[...] = proj(0, wb, wum_ref)
    qm_ref[...] = (proj(wb, wb + wm, wum_ref) * scale).astype(BF16)


def _inproj_fill_kernel(h_ref, wqkvf_ref, wum_ref, bf_ref, kall_ref, vall_ref, *out_refs, **kw):
    del kall_ref, vall_ref
    _inproj_kernel(h_ref, wqkvf_ref, wum_ref, bf_ref, *out_refs, **kw)


def inproj(h, w_qkvf, w_um, layer, bf, wa, wb, wm, nh, n_seq, q_scale, kv_all):
    M, D = h.shape
    depth = w_qkvf.shape[0]
    L = M // n_seq
    bm = _tile(L, 512) if L >= SUBLANES else M
    tiles_per_seq = max(L // bm, 1)
    seqs_per_tile = max(bm // L, 1)
    rows_per_seq = min(bm, L)
    row = lambda n: pl.BlockSpec((bm, n), lambda i: (i, 0))
    heads = pl.BlockSpec((seqs_per_tile, nh, rows_per_seq, HEAD_DIM),
                         lambda i: (i // tiles_per_seq, 0, i % tiles_per_seq, 0))
    slab = pl.BlockSpec((None, seqs_per_tile, nh, rows_per_seq, HEAD_DIM),
                        lambda i: (layer, i // tiles_per_seq, 0, i % tiles_per_seq, 0))
    hshape = lambda dt: jax.ShapeDtypeStruct((n_seq, nh, L, HEAD_DIM), dt)
    all_shape = jax.ShapeDtypeStruct((depth, n_seq, nh, L, HEAD_DIM), F32)
    in_specs = [row(D), _resident((None, D, w_qkvf.shape[2]), lambda i: (layer, 0, 0)),
                _resident((None, D, w_um.shape[2]), lambda i: (layer, 0, 0)),
                pl.BlockSpec((1, LANES), lambda i: (0, 0))] + [pl.BlockSpec(memory_space=pl.ANY)] * 2
    return pl.pallas_call(
        functools.partial(_inproj_fill_kernel, wa=wa, wb=wb, wm=wm, nh=nh, q_scale=q_scale),
        grid=(M // bm,), in_specs=in_specs,
        out_specs=[heads, slab, slab, heads, heads, row(wb), row(wm), row(nh)],
        out_shape=[hshape(BF16), all_shape, all_shape, hshape(BF16), hshape(BF16),
                   jax.ShapeDtypeStruct((M, wb), F32), jax.ShapeDtypeStruct((M, wm), BF16),
                   jax.ShapeDtypeStruct((M, nh), F32)],
        input_output_aliases={4: 1, 5: 2},
        compiler_params=_params(("parallel",)), name="inproj",
    )(h, w_qkvf, w_um, bf, *kv_all)


def _fox_bias_kernel(x_ref, qa_ref, ka_ref):
    x = x_ref[0]
    L = x.shape[-1]
    nh = qa_ref.shape[1]
    lane = lax.broadcasted_iota(jnp.int32, x.shape, 1)
    s = 1
    while s < L:
        x = x + jnp.where(lane >= s, pltpu.roll(x, s, axis=1), 0.0)
        s *= 2
    t = jnp.transpose(jnp.concatenate([x * LOG2E, jnp.zeros((LANES - x.shape[0], L), F32)], axis=0))

    col = lax.broadcasted_iota(jnp.int32, t.shape, 1)
    for h in range(nh):
        c = jnp.broadcast_to(t[:, h:h + 1], t.shape)
        hi = _top_half(c)
        mid = _top_half(c - hi)
        lo = c - hi - mid
        ones = lambda a, b: jnp.where((col >= a) & (col < b), 1.0, 0.0)
        qa = jnp.where(col == 0, hi, jnp.where(col == 1, mid, jnp.where(col == 2, lo, ones(3, 6))))
        ka = jnp.where(col == 3, -hi, jnp.where(col == 4, -mid, jnp.where(col == 5, -lo, ones(0, 3))))
        qa_ref[0, h] = qa.astype(BF16)
        ka_ref[0, h] = ka.astype(BF16)


def fox_bias_features(logf, nh):
    B, HP, L = logf.shape
    out = pl.BlockSpec((1, nh, L, HEAD_DIM), lambda b: (b, 0, 0, 0))
    return pl.pallas_call(
        _fox_bias_kernel, grid=(B,),
        in_specs=[pl.BlockSpec((1, HP, L), lambda b: (b, 0, 0))],
        out_specs=[out, out],
        out_shape=[jax.ShapeDtypeStruct((B, nh, L, HEAD_DIM), BF16)] * 2,
        compiler_params=_params(("parallel",)), name="fox_bias",
    )(logf)


def _fox_kernel(q_ref, qa_ref, k_ref, ka_ref, v_ref, o_ref, m_sc, l_sc, acc_sc):
    qi = pl.program_id(1)
    ki = pl.program_id(2)
    nh = q_ref.shape[0]

    @pl.when(ki == 0)
    def _():
        m_sc[...] = jnp.full(m_sc.shape, -jnp.inf, F32)
        l_sc[...] = jnp.zeros(l_sc.shape, F32)
        acc_sc[...] = jnp.zeros(acc_sc.shape, F32)

    def update(masked):
        heads = range(nh)
        s = [_dot_nt(jnp.concatenate([q_ref[h], qa_ref[h]], axis=1),
                     jnp.concatenate([k_ref[h], ka_ref[h]], axis=1)) for h in heads]
        if masked:
            row = lax.broadcasted_iota(jnp.int32, s[0].shape, 0)
            col = lax.broadcasted_iota(jnp.int32, s[0].shape, 1)
            s = [jnp.where(col <= row, sh, -jnp.inf) for sh in s]
        m_prev = [m_sc[h] for h in heads]
        m_new = [jnp.maximum(m_prev[h], jnp.max(s[h], axis=-1, keepdims=True)) for h in heads]
        alpha = [jnp.exp2(m_prev[h] - m_new[h]) for h in heads]
        p = [jnp.exp2(s[h] - m_new[h]) for h in heads]
        for h in heads:
            l_sc[h] = alpha[h] * l_sc[h] + jnp.sum(p[h], axis=-1, keepdims=True)
            m_sc[h] = m_new[h]
        pv = [_dot(p[h].astype(BF16), v_ref[h]) for h in heads]
        for h in heads:
            acc_sc[h] = alpha[h] * acc_sc[h] + pv[h]

    @pl.when(ki < qi)
    def _():
        update(False)

    @pl.when(ki == qi)
    def _():
        update(True)
        for h in range(nh):
            o_ref[:, h * HEAD_DIM:(h + 1) * HEAD_DIM] = (acc_sc[h] / l_sc[h]).astype(o_ref.dtype)


def fox_prompt(q, qa, k, ka, v):
    B, nh, L, _ = q.shape
    bq = _tile(L, 512)
    nq = L // bq
    qspec = pl.BlockSpec((None, nh, bq, HEAD_DIM), lambda b, qi, ki: (b, 0, qi, 0))
    kspec = pl.BlockSpec((None, nh, bq, HEAD_DIM), lambda b, qi, ki: (b, 0, jnp.minimum(ki, qi), 0))
    return pl.pallas_call(
        _fox_kernel, grid=(B, nq, nq),
        in_specs=[qspec, qspec, kspec, kspec, kspec],
        out_specs=pl.BlockSpec((None, bq, nh * HEAD_DIM), lambda b, qi, ki: (b, qi, 0)),
        out_shape=jax.ShapeDtypeStruct((B, L, nh * HEAD_DIM), BF16),
        scratch_shapes=[pltpu.VMEM((nh, bq, 1), F32), pltpu.VMEM((nh, bq, 1), F32),
                        pltpu.VMEM((nh, bq, HEAD_DIM), F32)],
        compiler_params=_params(("parallel", "parallel", "arbitrary")), name="fox_prompt",
    )(q, qa, k, ka, v)


def _fox_decode_kernel(pt_ref, q_ref, lx_ref, kn_ref, vn_ref, *refs, pps, n_new):
    kp = refs[:pps]
    vp = refs[pps:2 * pps]
    lp = refs[2 * pps:3 * pps]
    o_ref, m_sc, l_sc, acc_sc, suf_sc = refs[3 * pps:]
    del pt_ref
    step = pl.program_id(1)
    nh, tp, _ = q_ref.shape
    page = kn_ref.shape[1]

    @pl.when(step == 0)
    def _():
        m_sc[...] = jnp.full(m_sc.shape, -jnp.inf, F32)
        l_sc[...] = jnp.zeros(l_sc.shape, F32)
        acc_sc[...] = jnp.zeros(acc_sc.shape, F32)
        suf_sc[...] = jnp.zeros(suf_sc.shape, F32)

    ck = lx_ref[...]
    col = lax.broadcasted_iota(jnp.int32, ck.shape, 1)
    row_t = lax.broadcasted_iota(jnp.int32, ck.shape, 0) % tp
    s = 1
    while s < n_new:
        ck = ck + jnp.where(col >= s, pltpu.roll(ck, s, axis=1), 0.0)
        s *= 2
    cq = jnp.sum(jnp.where(col == row_t, ck, 0.0), axis=-1, keepdims=True)

    def update(sc, v_tiles):
        m_prev = m_sc[...]
        m_new = jnp.maximum(m_prev, jnp.max(sc, axis=-1, keepdims=True))
        alpha = jnp.exp(m_prev - m_new)
        p = jnp.exp(sc - m_new)
        l_sc[...] = alpha * l_sc[...] + jnp.sum(p, axis=-1, keepdims=True)
        pv = []
        for h in range(nh):
            ph = p[h * tp:(h + 1) * tp].astype(BF16)
            acc = _dot(ph[:, :page], v_tiles[0](h))
            for i in range(1, len(v_tiles)):
                acc += _dot(ph[:, i * page:(i + 1) * page], v_tiles[i](h))
            pv.append(acc)
        acc_sc[...] = alpha * acc_sc[...] + jnp.concatenate(pv, axis=0)
        m_sc[...] = m_new

    lane = lax.broadcasted_iota(jnp.int32, (HEADS_PAD, LANES), 1)
    after = suf_sc[...]
    biases = []
    for i in range(pps):
        lf = lp[i][...]
        inc = lf
        s = 1
        while s < LANES:
            inc = inc + jnp.where(lane < LANES - s, pltpu.roll(inc, LANES - s, axis=1), 0.0)
            s *= 2
        biases.append(inc - lf + after)
        after = after + jnp.broadcast_to(inc[:, 0:1], after.shape)
    suf_sc[...] = after
    bias = jnp.concatenate(biases, axis=1)

    rows = []
    for h in range(nh):
        qh = q_ref[h]
        sh = jnp.concatenate([_dot_nt(qh, kp[i][h].astype(BF16)) for i in range(pps)], axis=1)
        rows.append(sh + bias[h:h + 1])
    update(jnp.concatenate(rows, axis=0) + cq,
           [lambda h, i=i: vp[i][h].astype(BF16) for i in range(pps)])

    @pl.when(step == pl.num_programs(1) - 1)
    def _():
        sc = jnp.concatenate([_dot_nt(q_ref[h], kn_ref[h]) for h in range(nh)], axis=0) + cq - ck
        sc = jnp.where((col <= row_t) & (col < n_new), sc, -jnp.inf)
        update(sc, [lambda h: vn_ref[h]])
        o_ref[...] = acc_sc[...] / l_sc[...]


def fox_decode(layer, page_table, q, lx, kn, vn, cache_k, cache_v, cache_lf, n_new):
    Bd, nh, tp, _ = q.shape
    n_pages = page_table.shape[1]
    page = cache_k.shape[3]
    pps = DECODE_PAGES_PER_STEP if n_pages % DECODE_PAGES_PER_STEP == 0 else 1
    steps = n_pages // pps
    R = nh * tp

    def page_spec(i, shape):
        zeros = (0,) * len(shape)
        return pl.BlockSpec((None, None) + shape,
                            lambda b, s, pt: (layer, pt[b, n_pages - 1 - (s * pps + i)]) + zeros)

    def per_b(shape):
        zeros = (0,) * len(shape)
        return pl.BlockSpec((None,) + shape, lambda b, s, pt: (b,) + zeros)

    in_specs = ([per_b((nh, tp, HEAD_DIM)), per_b((R, LANES)), per_b((nh, page, HEAD_DIM)),
                 per_b((nh, page, HEAD_DIM))]
                + [page_spec(i, (nh, page, HEAD_DIM)) for i in range(pps)]
                + [page_spec(i, (nh, page, HEAD_DIM)) for i in range(pps)]
                + [page_spec(i, (HEADS_PAD, page)) for i in range(pps)])
    grid_spec = pltpu.PrefetchScalarGridSpec(
        num_scalar_prefetch=1, grid=(Bd, steps), in_specs=in_specs,
        out_specs=per_b((R, HEAD_DIM)),
        scratch_shapes=[pltpu.VMEM((R, 1), F32), pltpu.VMEM((R, 1), F32), pltpu.VMEM((R, HEAD_DIM), F32),
                        pltpu.VMEM((HEADS_PAD, LANES), F32)])
    return pl.pallas_call(
        functools.partial(_fox_decode_kernel, pps=pps, n_new=n_new), grid_spec=grid_spec,
        out_shape=jax.ShapeDtypeStruct((Bd, R, HEAD_DIM), F32),
        compiler_params=_params(("parallel", "arbitrary")), name="fox_decode",
    )(page_table, q, lx, kn, vn, *([cache_k] * pps), *([cache_v] * pps), *([cache_lf] * pps))


def _s5_disc_kernel(are_ref, aim_ref, ldt_ref, bre_ref, bim_ref, abre_ref, abim_ref, bbre_ref, bbim_ref):
    a_re = are_ref[0]
    a_im = aim_ref[0]
    dt = jnp.exp(ldt_ref[0])
    mag = jnp.exp(a_re * dt)
    ab_re = mag * jnp.cos(a_im * dt)
    ab_im = mag * jnp.sin(a_im * dt)
    den = a_re * a_re + a_im * a_im
    n_re = ab_re - 1.0
    q_re = (n_re * a_re + ab_im * a_im) / den
    q_im = (ab_im * a_re - n_re * a_im) / den
    b_re = bre_ref[0]
    b_im = bim_ref[0]
    abre_ref[0] = ab_re
    abim_ref[0] = ab_im
    bbre_ref[0] = q_re * b_re - q_im * b_im
    bbim_ref[0] = q_re * b_im + q_im * b_re


def s5_discretize(a_re, a_im, log_dt, b_re_t, b_im_t):
    Dp, G, P = a_re.shape
    C = b_re_t.shape[2]
    gp = pl.BlockSpec((1, G, 1, P), lambda l: (l, 0, 0, 0))
    gcp = pl.BlockSpec((1, G, C, P), lambda l: (l, 0, 0, 0))
    return pl.pallas_call(
        _s5_disc_kernel, grid=(Dp,),
        in_specs=[gp, gp, pl.BlockSpec((1, G, 1, 1), lambda l: (l, 0, 0, 0)), gcp, gcp],
        out_specs=[gp, gp, gcp, gcp],
        out_shape=[jax.ShapeDtypeStruct((Dp, G, 1, P), F32)] * 2 + [jax.ShapeDtypeStruct((Dp, G, C, P), F32)] * 2,
        compiler_params=_params(("parallel",)), name="s5_discretize",
    )(a_re.reshape(Dp, G, 1, P), a_im.reshape(Dp, G, 1, P), log_dt.reshape(Dp, G, 1, 1), b_re_t, b_im_t)


def _s5_kernel(u_ref, h0re_ref, h0im_ref, are_ref, aim_ref, wbu_ref, wcre_ref, wcim_ref, d_ref, wglu_ref,
               o_ref, hre_ref, him_ref, bre_sc, bim_sc, *, nb, lane_chunk, batch_major):
    n = pl.program_id(0)
    rows = bre_sc.shape[0]
    tc = rows // nb
    gp = bre_sc.shape[1]
    kc = wbu_ref.shape[1]
    sc = wbu_ref.shape[2] // 2
    n_chunks = wbu_ref.shape[0]

    @pl.when(n == 0)
    def _():
        hre_ref[...] = h0re_ref[...]
        him_ref[...] = h0im_ref[...]

    if batch_major:
        r_i = lax.broadcasted_iota(jnp.int32, (rows, tc), 0)
        t_i = lax.broadcasted_iota(jnp.int32, (rows, tc), 1)
        spread, terms = [], []
        for b in range(nb):
            x = u_ref[b]
            hi = _top_half(x)
            mid = _top_half(x - hi)
            terms += [hi.astype(BF16), mid.astype(BF16), (x - hi - mid).astype(BF16)]
            spread += [(r_i == t_i * nb + b).astype(BF16)] * 3
        u = _dot(jnp.concatenate(spread, axis=1), jnp.concatenate(terms, axis=0))
    else:
        u = u_ref[...]
    ub = u.astype(BF16)
    for c in range(n_chunks):
        bu = _dot(ub[:, c * kc:(c + 1) * kc], wbu_ref[c])
        bre_sc[:, c * sc:(c + 1) * sc] = bu[:, :sc]
        bim_sc[:, c * sc:(c + 1) * sc] = bu[:, sc:]

    spt = SUBLANES // nb
    for c in range(gp // lane_chunk):
        cs = slice(c * lane_chunk, (c + 1) * lane_chunk)
        ar = jnp.broadcast_to(are_ref[:, cs], (SUBLANES, lane_chunk))
        ai = jnp.broadcast_to(aim_ref[:, cs], (SUBLANES, lane_chunk))
        band = lax.broadcasted_iota(jnp.int32, (SUBLANES, lane_chunk), 0) // nb

        def step(t, carry, cs=cs, ar=ar, ai=ai, band=band):
            hr, hi = carry
            r0 = pl.multiple_of(t * SUBLANES, SUBLANES)
            xr = bre_sc[pl.ds(r0, SUBLANES), cs]
            xi = bim_sc[pl.ds(r0, SUBLANES), cs]
            out_r = out_i = None
            for k in range(spt):
                hr, hi = ar * hr - ai * hi + xr, ar * hi + ai * hr + xi
                out_r = hr if k == 0 else jnp.where(band == k, hr, out_r)
                out_i = hi if k == 0 else jnp.where(band == k, hi, out_i)
                if spt > 1:
                    hr = pltpu.roll(hr, nb, axis=0)
                    hi = pltpu.roll(hi, nb, axis=0)
            bre_sc[pl.ds(r0, SUBLANES), cs] = out_r
            bim_sc[pl.ds(r0, SUBLANES), cs] = out_i
            return hr, hi

        init = lambda ref: jnp.concatenate([ref[:, cs]] * spt, axis=0)
        hr, hi = lax.fori_loop(0, rows // SUBLANES, step, (init(hre_ref), init(him_ref)))
        hre_ref[:, cs] = hr[:nb]
        him_ref[:, cs] = hi[:nb]

    ys = []
    for c in range(n_chunks):
        hr = bre_sc[:, c * sc:(c + 1) * sc].astype(BF16)
        hi = bim_sc[:, c * sc:(c + 1) * sc].astype(BF16)
        ys.append(_dot(hr, wcre_ref[c]) - _dot(hi, wcim_ref[c]))
    y = jnp.concatenate(ys, axis=1) + d_ref[...] * u
    z = _gelu_tanh(y)
    o = (z * _sigmoid(_dot(z.astype(BF16), wglu_ref[...]))).astype(BF16)
    if batch_major:
        i_i = lax.broadcasted_iota(jnp.int32, (rows, rows), 0)
        r_i = lax.broadcasted_iota(jnp.int32, (rows, rows), 1)
        gather = (r_i == (i_i % tc) * nb + i_i // tc).astype(BF16)
        ob = _dot(gather, o).astype(BF16)
        for b in range(nb):
            o_ref[b] = ob[b * tc:(b + 1) * tc]
    else:
        o_ref[...] = o


def s5_branch(u, h0_re, h0_im, ab_re, ab_im, wbu, wc_re, wc_im, d_skip, w_glu, nb):
    batch_major = u.ndim == 3
    L = u.shape[1] if batch_major else u.shape[0] // nb
    wb = u.shape[-1]
    gp = h0_re.shape[1]
    tc = _tile(L, 128)
    rows = tc * nb
    lane_chunk = _tile(gp, 768)
    full = lambda a: _resident(a.shape, lambda n: (0,) * a.ndim)
    st = pl.BlockSpec((nb, gp), lambda n: (0, 0))
    io = (pl.BlockSpec((nb, tc, wb), lambda n: (0, n, 0)) if batch_major
          else pl.BlockSpec((rows, wb), lambda n: (n, 0)))
    return pl.pallas_call(
        functools.partial(_s5_kernel, nb=nb, lane_chunk=lane_chunk, batch_major=batch_major), grid=(L // tc,),
        in_specs=[io, st, st, full(ab_re), full(ab_im),
                  full(wbu), full(wc_re), full(wc_im), full(d_skip), full(w_glu)],
        out_specs=[io, st, st],
        out_shape=[jax.ShapeDtypeStruct(u.shape, BF16), jax.ShapeDtypeStruct((nb, gp), F32),
                   jax.ShapeDtypeStruct((nb, gp), F32)],
        scratch_shapes=[pltpu.VMEM((rows, gp), F32), pltpu.VMEM((rows, gp), F32)],
        compiler_params=_params(("arbitrary",)), name="s5_branch",
    )(u, h0_re, h0_im, ab_re, ab_im, wbu, wc_re, wc_im, d_skip, w_glu)


def _mem_kv_kernel(x_ref, g_ref, w_ref, k_ref, v_ref, kb_ref, vb_ref):
    h = _rms(x_ref[...], g_ref[0]).astype(BF16)
    z = _dot(h, w_ref[0])
    wm = k_ref.shape[-1]
    k_ref[0] = z[:, :wm]
    v_ref[0] = z[:, wm:]
    kb_ref[0] = z[:, :wm].astype(BF16)
    vb_ref[0] = z[:, wm:].astype(BF16)


def mem_kv(mem, g_mem, w_mkv):
    R, D = mem.shape
    Dp, _, W2 = w_mkv.shape
    wm = W2 // 2
    bm = _tile(R, 512)
    o = pl.BlockSpec((1, bm, wm), lambda l, i: (l, i, 0))
    return pl.pallas_call(
        _mem_kv_kernel, grid=(Dp, R // bm),
        in_specs=[pl.BlockSpec((bm, D), lambda l, i: (i, 0)), pl.BlockSpec((1, 1, D), lambda l, i: (l, 0, 0)),
                  pl.BlockSpec((1, D, W2), lambda l, i: (l, 0, 0))],
        out_specs=[o, o, o, o],
        out_shape=[jax.ShapeDtypeStruct((Dp, R, wm), F32)] * 2 + [jax.ShapeDtypeStruct((Dp, R, wm), BF16)] * 2,
        compiler_params=_params(("parallel", "arbitrary")), name="mem_kv",
    )(mem, g_mem, w_mkv)


def _mem_attn_kernel(q_ref, k_ref, v_ref, o_ref):
    nh = q_ref.shape[-1] // HEAD_DIM
    for h in range(nh):
        hs = slice(h * HEAD_DIM, (h + 1) * HEAD_DIM)
        s = _dot_nt(q_ref[:, hs], k_ref[:, hs])
        p = jnp.exp(s - jnp.max(s, axis=-1, keepdims=True))
        p = p / jnp.sum(p, axis=-1, keepdims=True)
        o_ref[:, hs] = _dot(p.astype(BF16), v_ref[:, hs]).astype(o_ref.dtype)


def mem_attn(q, k, v):
    B, L, W = q.shape
    Nm = k.shape[1]
    bq = _tile(L, 512)
    qs = pl.BlockSpec((None, bq, W), lambda b, i: (b, i, 0))
    ks = pl.BlockSpec((None, Nm, W), lambda b, i: (b, 0, 0))
    return pl.pallas_call(
        _mem_attn_kernel, grid=(B, L // bq), in_specs=[qs, ks, ks], out_specs=qs,
        out_shape=jax.ShapeDtypeStruct((B, L, W), BF16),
        compiler_params=_params(("parallel", "parallel")), name="mem_attn",
    )(q, k, v)


def _merge_kernel(h_ref, oa_ref, ob_ref, om_ref, g0_ref, g1_ref, g2_ref, pa_ref, pb_ref, pm_ref, o_ref):
    h = h_ref[...]
    acc = _sigmoid(_dot(h, g0_ref[...])) * _dot(oa_ref[...], pa_ref[...])
    acc += _sigmoid(_dot(h, g1_ref[...])) * _dot(ob_ref[...], pb_ref[...])
    acc += _sigmoid(_dot(h, g2_ref[...])) * _dot(om_ref[...], pm_ref[...])
    o_ref[...] = acc.astype(o_ref.dtype)


def merge(h, oa, ob, om, w, layer, pa, pb, pm):
    M, D = h.shape
    bm = _tile(M, 1024)
    bn = _tile(D, 512)
    nj = D // bn
    rowspec = lambda a: pl.BlockSpec((bm, a.shape[1]), lambda j, i: (i, 0))
    colspec = lambda a: pl.BlockSpec((a.shape[0], bn), lambda j, i: (0, j))
    gate = lambda b: pl.BlockSpec((None, D, bn), lambda j, i: (layer, 0, b * nj + j))
    return pl.pallas_call(
        _merge_kernel, grid=(nj, M // bm),
        in_specs=[rowspec(h), rowspec(oa), rowspec(ob), rowspec(om), gate(0), gate(1), gate(2),
                  colspec(pa), colspec(pb), colspec(pm)],
        out_specs=pl.BlockSpec((bm, bn), lambda j, i: (i, j)),
        out_shape=jax.ShapeDtypeStruct((M, D), BF16),
        compiler_params=_params(("parallel", "arbitrary")), name="merge",
    )(h, oa, ob, om, w, w, w, pa, pb, pm)


def _proj_res_kernel(a_ref, w_ref, x_ref, gpost_ref, gnext_ref, xo_ref, *h_ref):
    f = _dot(a_ref[...], w_ref[...])
    x = x_ref[...] + _rms(f, gpost_ref[...])
    xo_ref[...] = x
    if h_ref:
        h_ref[0][...] = _rms(x, gnext_ref[...]).astype(BF16)


def proj_residual(a, w, x, g_post, g_next, emit_h=True):
    M, K = a.shape
    D = w.shape[1]
    bm = _tile(M, 256)
    row = lambda n: pl.BlockSpec((bm, n), lambda i: (i, 0))
    g = pl.BlockSpec((1, D), lambda i: (0, 0))
    out_specs = [row(D)] + ([row(D)] if emit_h else [])
    out_shape = [jax.ShapeDtypeStruct((M, D), F32)] + ([jax.ShapeDtypeStruct((M, D), BF16)] if emit_h else [])
    res = pl.pallas_call(
        _proj_res_kernel, grid=(M // bm,),
        in_specs=[row(K), _resident((K, D), lambda i: (0, 0)), row(D), g, g],
        out_specs=out_specs, out_shape=out_shape,
        compiler_params=_params(("parallel",)), name="proj_residual",
    )(a, w, x, g_post.reshape(1, D), g_next.reshape(1, D))
    return (res[0], res[1]) if emit_h else (res[0], None)


def _ffn_up_kernel(h_ref, hd_ref, wa_ref, wb_ref, sta_ref, stb_ref, cwa_ref, cwb_ref, cba_ref, cbb_ref,
                   act_ref, sa_ref, sb_ref, actd_ref, sad_ref, sbd_ref, wa_sc, wb_sc, ca_sc, cb_sc,
                   *, tiles_per_seq, n_dec):
    i = pl.program_id(1)
    bm = h_ref.shape[0]

    def taps(u, u1, u2, cw_ref, cb_ref):
        return cb_ref[...] + cw_ref[2:3] * u + cw_ref[1:2] * u1 + cw_ref[0:1] * u2

    @pl.when(i == 0)
    def _():
        wa_sc[...] = wa_ref[...].astype(BF16)
        wb_sc[...] = wb_ref[...].astype(BF16)
        hd = hd_ref[...]
        n_new = hd.shape[0]

        def conv_dec(w_sc, st_ref, cw_ref, cb_ref, s_ref):
            u = _dot(hd, w_sc[...])
            ext = jnp.concatenate([st_ref[...], u], axis=0)
            s_ref[...] = ext[n_new:]
            return taps(u, ext[n_dec:n_dec + n_new], ext[:n_new], cw_ref, cb_ref)

        ya = conv_dec(wa_sc, sta_ref, cwa_ref, cba_ref, sad_ref)
        yb = conv_dec(wb_sc, stb_ref, cwb_ref, cbb_ref, sbd_ref)
        actd_ref[...] = (_gelu_tanh(ya) * yb).astype(actd_ref.dtype)

    @pl.when(i % tiles_per_seq == 0)
    def _():
        ca_sc[...] = jnp.zeros(ca_sc.shape, F32)
        cb_sc[...] = jnp.zeros(cb_sc.shape, F32)

    def conv(w_sc, c_sc, cw_ref, cb_ref, s_ref):
        u = _dot(h_ref[...], w_sc[...])
        prev = c_sc[...]
        row = lax.broadcasted_iota(jnp.int32, u.shape, 0)
        u1 = jnp.where(row == 0, prev[SUBLANES - 1:SUBLANES], pltpu.roll(u, 1, axis=0))
        u2 = jnp.where(row == 0, prev[SUBLANES - 2:SUBLANES - 1],
                       jnp.where(row == 1, prev[SUBLANES - 1:SUBLANES], pltpu.roll(u, 2, axis=0)))
        c_sc[...] = u[bm - SUBLANES:]
        s_ref[...] = u[bm - 2:]
        return taps(u, u1, u2, cw_ref, cb_ref)

    ya = conv(wa_sc, ca_sc, cwa_ref, cba_ref, sa_ref)
    yb = conv(wb_sc, cb_sc, cwb_ref, cbb_ref, sb_ref)
    act_ref[...] = (_gelu_tanh(ya) * yb).astype(act_ref.dtype)


def ffn_up(h, h_dec, w_up, layer, conv_w, conv_b, state_dec, n_seq, n_dec):
    M, D = h.shape
    Md = h_dec.shape[0]
    F2 = w_up.shape[2]
    F = F2 // 2
    L = M // n_seq
    bm = _tile(L, 1024)
    bn = _tile(F, 512)
    nj = F // bn
    tps = L // bm
    a_col = lambda r: pl.BlockSpec((r, bn), lambda j, i: (0, j))
    b_col = lambda r: pl.BlockSpec((r, bn), lambda j, i: (0, nj + j))
    wa = pl.BlockSpec((None, D, bn), lambda j, i: (layer, 0, j))
    wb = pl.BlockSpec((None, D, bn), lambda j, i: (layer, 0, nj + j))
    st = pl.BlockSpec((None, 2, bn), lambda j, i: (i // tps, 0, j))
    return pl.pallas_call(
        functools.partial(_ffn_up_kernel, tiles_per_seq=tps, n_dec=n_dec), grid=(nj, M // bm),
        in_specs=[pl.BlockSpec((bm, D), lambda j, i: (i, 0)), pl.BlockSpec((Md, D), lambda j, i: (0, 0)), wa, wb,
                  a_col(2 * n_dec), b_col(2 * n_dec), a_col(3), b_col(3), a_col(1), b_col(1)],
        out_specs=[pl.BlockSpec((bm, bn), lambda j, i: (i, j)), st, st,
                   a_col(Md), a_col(2 * n_dec), a_col(2 * n_dec)],
        out_shape=[jax.ShapeDtypeStruct((M, F), BF16), jax.ShapeDtypeStruct((n_seq, 2, F), F32),
                   jax.ShapeDtypeStruct((n_seq, 2, F), F32), jax.ShapeDtypeStruct((Md, F), BF16),
                   jax.ShapeDtypeStruct((2 * n_dec, F), F32), jax.ShapeDtypeStruct((2 * n_dec, F), F32)],
        scratch_shapes=[pltpu.VMEM((D, bn), BF16), pltpu.VMEM((D, bn), BF16),
                        pltpu.VMEM((SUBLANES, bn), F32), pltpu.VMEM((SUBLANES, bn), F32)],
        compiler_params=_params(("parallel", "arbitrary")), name="ffn_up",
    )(h, h_dec, w_up, w_up, state_dec, state_dec, conv_w, conv_w, conv_b, conv_b)


def _s5_weights(bb_re_t, bb_im_t, c_re, c_im):
    G, C, P = bb_re_t.shape
    gc = S5_GROUPS_PER_CHUNK if G % S5_GROUPS_PER_CHUNK == 0 else G
    n = G // gc
    eye = jnp.eye(gc, dtype=F32)

    def bu(bb):
        return jnp.einsum('ngcp,gh->ngchp', bb.reshape(n, gc, C, P), eye).reshape(n, gc * C, gc * P)

    def cy(c):
        return jnp.einsum('ngcp,gh->ngphc', c.reshape(n, gc, C, P), eye).reshape(n, gc * P, gc * C)

    wbu = jnp.concatenate([bu(bb_re_t), bu(bb_im_t)], axis=-1).astype(BF16)
    return wbu, cy(c_re).astype(BF16), cy(c_im).astype(BF16)


def _mixer_block(x, h, wl, n_seq, q_scale, kv_all, branch_fn):
    q, k, v, kb, vb, u, qm, lf = inproj(h, wl['w_qkvf'], wl['w_um'], wl['layer'], wl['b_f'], wl['wa'], wl['wb'],
                                        wl['wm'], wl['nh'], n_seq, q_scale, kv_all)
    o_a, o_b, o_m, extra = branch_fn(q, kb, vb, lf, u, qm)
    merged = merge(h, o_a, o_b, o_m, wl['w_gate'], wl['layer'], wl['p_a'], wl['p_b'], wl['p_m'])
    x, h2 = proj_residual(merged, wl['w_out'], x, wl['g_post_mix'], wl['g_pre_ffn'])
    return x, h2, k, v, lf, extra


def kernel(x_prompt, x_sample, cache_k, cache_v, cache_logf, cache_mem_k, cache_mem_v, state_ssm_re, state_ssm_im, state_conv, page_table, mem_prompt, w_in, b_f, a_re, a_im, log_dt, b_re, b_im, c_re, c_im, d_skip, w_glu, g_mem, w_mkv, p_a, p_b, p_m, w_out, g_pre_mix, g_post_mix, g_pre_ffn, g_post_ffn, w_up, conv_w, conv_b, w_down):
    Bp, L, D = x_prompt.shape
    Bd, T, _ = x_sample.shape
    depth = w_in.shape[0]
    wa = p_a.shape[1]
    wb = p_b.shape[1]
    wm = p_m.shape[1]
    nh = b_f.shape[1]
    nhm = wm // HEAD_DIM
    G, P = a_re.shape[1:]
    C = b_re.shape[3]
    gp = G * P
    n_mem = mem_prompt.shape[1]
    n_pool, page = cache_k.shape[1:3]
    F2 = w_up.shape[2]

    o_u = 3 * wa + nh
    o_g = o_u + wb + wm
    w_qkvf = jnp.pad(w_in[:, :, :o_u], ((0, 0), (0, 0), (0, LANES - nh))).astype(BF16)
    w_um = w_in[:, :, o_u:o_g].astype(BF16)
    w_gate = w_in[:, :, o_g:].astype(BF16)
    b_f_pad = jnp.pad(b_f, ((0, 0), (0, LANES - nh))).reshape(depth, 1, LANES)
    ab_re, ab_im, bb_re_t, bb_im_t = s5_discretize(a_re, a_im, log_dt, b_re.transpose(0, 1, 3, 2),
                                                   b_im.transpose(0, 1, 3, 2))
    bf = lambda a: a.astype(BF16)
    w_mkv_b, p_a_b, p_b_b, p_m_b, w_out_b, w_down_b, w_glu_b = map(
        bf, (w_mkv, p_a, p_b, p_m, w_out, w_down, w_glu))

    mk, mv, mkb, mvb = mem_kv(mem_prompt.reshape(Bp * n_mem, D), g_mem.reshape(depth, 1, D), w_mkv_b)

    cache_k2 = cache_k.transpose(0, 1, 3, 2, 4)
    cache_v2 = cache_v.transpose(0, 1, 3, 2, 4)
    cache_lf_t = jnp.pad(cache_logf.transpose(0, 1, 3, 2), ((0, 0), (0, 0), (0, HEADS_PAD - nh), (0, 0)))
    cache_mk_b = bf(cache_mem_k).reshape(depth, Bd, n_mem, wm)
    cache_mv_b = bf(cache_mem_v).reshape(depth, Bd, n_mem, wm)

    xp = x_prompt.reshape(Bp * L, D)
    xs = x_sample.reshape(Bd * T, D)
    hp = rmsnorm_bf16(xp, g_pre_mix[0])
    hs = rmsnorm_bf16(xs, g_pre_mix[0])
    zeros_p = jnp.zeros((Bp, gp), F32)

    outs = [[] for _ in range(14)]
    kv_p = tuple(jnp.zeros((depth, Bp, nh, L, HEAD_DIM), F32) for _ in range(2))
    kv_s = tuple(jnp.zeros((depth, Bd, nh, T, HEAD_DIM), F32) for _ in range(2))
    for l in range(depth):
        wbu, wc_re, wc_im = _s5_weights(bb_re_t[l], bb_im_t[l], c_re[l], c_im[l])
        wl = dict(w_qkvf=w_qkvf, w_um=w_um, w_gate=w_gate, layer=l, b_f=b_f_pad[l], wa=wa, wb=wb, wm=wm, nh=nh,
                  p_a=p_a_b[l], p_b=p_b_b[l], p_m=p_m_b[l], w_out=w_out_b[l], w_down=w_down_b[l],
                  g_post_mix=g_post_mix[l], g_pre_ffn=g_pre_ffn[l], g_post_ffn=g_post_ffn[l],
                  g_next=g_pre_mix[(l + 1) % depth])
        s5_args = (ab_re[l].reshape(1, gp), ab_im[l].reshape(1, gp), wbu, wc_re, wc_im,
                   d_skip[l].reshape(1, wb), w_glu_b[l])
        emit_h = l + 1 < depth

        def prompt_branches(q, kb, vb, lf, u, qm):
            lf_t = jnp.pad(lf.reshape(Bp, L, nh).transpose(0, 2, 1), ((0, 0), (0, HEADS_PAD - nh), (0, 0)))
            qa, ka = fox_bias_features(lf_t, nh)
            o_a = fox_prompt(q, qa, kb, ka, vb).reshape(Bp * L, wa)
            o_b, hre, him = s5_branch(u.reshape(Bp, L, wb), zeros_p, zeros_p, *s5_args, nb=Bp)
            o_b = o_b.reshape(Bp * L, wb)
            o_m = mem_attn(qm.reshape(Bp, L, wm), mkb[l].reshape(Bp, n_mem, wm),
                           mvb[l].reshape(Bp, n_mem, wm)).reshape(Bp * L, wm)
            return o_a, o_b, o_m, (hre, him)

        xp, h2p, k, v, lf, (hre, him) = _mixer_block(xp, hp, wl, Bp, HEAD_DIM ** -0.5 * LOG2E, kv_p,
                                                      prompt_branches)
        kv_p = (k, v)
        for i, a in zip((2, 3, 4, 5, 6), (
                lf.reshape(Bp, L, nh),
                mk[l].reshape(Bp, n_mem, nhm, HEAD_DIM), mv[l].reshape(Bp, n_mem, nhm, HEAD_DIM),
                hre.reshape(Bp, G, P), him.reshape(Bp, G, P))):
            outs[i].append(a)

        def sample_branches(q, kb, vb, lf, u, qm):
            tp = -(-T // SUBLANES) * SUBLANES
            padrows = lambda a, n: jnp.pad(a, ((0, 0), (0, 0), (0, n - T), (0, 0)))
            lx = jnp.pad(lf.reshape(Bd, T, nh).transpose(0, 2, 1), ((0, 0), (0, 0), (0, LANES - T)))
            lx = jnp.broadcast_to(lx[:, :, None], (Bd, nh, tp, LANES)).reshape(Bd, nh * tp, LANES)
            o = fox_decode(l, page_table, padrows(q, tp), lx, padrows(kb, page), padrows(vb, page),
                           cache_k2, cache_v2, cache_lf_t, T)
            o_a = o.reshape(Bd, nh, tp, HEAD_DIM)[:, :, :T].transpose(0, 2, 1, 3).reshape(Bd * T, wa).astype(BF16)
            u_tb = u.reshape(Bd, T, wb).transpose(1, 0, 2).reshape(T * Bd, wb)
            o_b, hre, him = s5_branch(u_tb, state_ssm_re[l].reshape(Bd, gp), state_ssm_im[l].reshape(Bd, gp),
                                      *s5_args, nb=Bd)
            o_b = o_b.reshape(T, Bd, wb).transpose(1, 0, 2).reshape(Bd * T, wb)
            o_m = mem_attn(qm.reshape(Bd, T, wm), cache_mk_b[l], cache_mv_b[l]).reshape(Bd * T, wm)
            return o_a, o_b, o_m, (hre, him)

        xs, h2s, k, v, lf, (hre, him) = _mixer_block(xs, hs, wl, Bd, HEAD_DIM ** -0.5, kv_s, sample_branches)
        kv_s = (k, v)
        for i, a in zip((10, 11, 12), (lf.reshape(Bd, T, nh), hre.reshape(Bd, G, P), him.reshape(Bd, G, P))):
            outs[i].append(a)

        h_tb = h2s.reshape(Bd, T, D).transpose(1, 0, 2).reshape(T * Bd, D)
        st_tb = state_conv[l].transpose(1, 0, 2).reshape(2 * Bd, F2)
        act_p, sa, sb, act_s, sad, sbd = ffn_up(h2p, h_tb, w_up, l, conv_w[l], conv_b[l].reshape(1, F2), st_tb, Bp, Bd)
        outs[7].append(jnp.concatenate([sa, sb], axis=-1))
        outs[13].append(jnp.concatenate([sad, sbd], axis=-1).reshape(2, Bd, F2).transpose(1, 0, 2))
        act_s = act_s.reshape(T, Bd, F2 // 2).transpose(1, 0, 2).reshape(Bd * T, F2 // 2)
        xp, hp = proj_residual(act_p, wl['w_down'], xp, wl['g_post_ffn'], wl['g_next'], emit_h=emit_h)
        xs, hs = proj_residual(act_s, wl['w_down'], xs, wl['g_post_ffn'], wl['g_next'], emit_h=emit_h)

    res = [jnp.stack(o) if o else None for o in outs]
    for i, a in zip((0, 1, 8, 9), kv_p + kv_s):
        res[i] = a.transpose(0, 1, 3, 2, 4)
    return (xp.reshape(Bp, L, D), xs.reshape(Bd, T, D)) + tuple(res)
```

```python
import functools
import math

import jax
import jax.numpy as jnp
from jax import lax
from jax.experimental import pallas as pl
from jax.experimental.pallas import tpu as pltpu

F32 = jnp.float32
BF16 = jnp.bfloat16

RMS_EPS = 1e-6
HEAD_DIM = 128
GROUP_CH = 16
LANES = 128
SUBLANES = 8
MXU_DIM = 256
VMEM_LIMIT = 56 * 1024 * 1024
S5_GROUPS_PER_CHUNK = MXU_DIM // GROUP_CH
HEADS_PAD = SUBLANES
LOG2E = 1.4426950408889634
DECODE_PAGES_PER_STEP = 16


def _tile(n, pref):
    return pref if n % pref == 0 else n


def _params(sem, vmem=VMEM_LIMIT):
    return pltpu.CompilerParams(dimension_semantics=sem, vmem_limit_bytes=vmem)


def _resident(shape, index_map):
    return pl.BlockSpec(shape, index_map, pipeline_mode=pl.Buffered(1))


def _dot(a, b):
    return jnp.dot(a, b, preferred_element_type=F32)


def _dot_nt(a, b):
    return lax.dot_general(a, b, (((1,), (1,)), ((), ())), preferred_element_type=F32)


def _rms(x, g):
    return x * lax.rsqrt(jnp.mean(x * x, axis=-1, keepdims=True) + RMS_EPS) * g


def _sigmoid(x):
    return 1.0 / (1.0 + jnp.exp(-x))


def _log_sigmoid(x):
    return jnp.minimum(x, 0.0) - jnp.log1p(jnp.exp(-jnp.abs(x)))


def _top_half(x):
    bits = lax.bitcast_convert_type(x, jnp.uint32) & jnp.uint32(0xFFFF0000)
    return lax.bitcast_convert_type(bits, F32)


def _gelu_tanh(x):
    c = math.sqrt(2.0 / math.pi)
    return 0.5 * x * (1.0 + jnp.tanh(c * (x + 0.044715 * (x * x * x))))


def _rmsnorm_kernel(x_ref, g_ref, o_ref):
    o_ref[...] = _rms(x_ref[...], g_ref[...]).astype(o_ref.dtype)


def rmsnorm_bf16(x, g):
    M, D = x.shape
    bm = _tile(M, 512)
    return pl.pallas_call(
        _rmsnorm_kernel, grid=(M // bm,),
        in_specs=[pl.BlockSpec((bm, D), lambda i: (i, 0)), pl.BlockSpec((1, D), lambda i: (0, 0))],
        out_specs=pl.BlockSpec((bm, D), lambda i: (i, 0)),
        out_shape=jax.ShapeDtypeStruct((M, D), BF16),
        compiler_params=_params(("parallel",)), name="rmsnorm",
    )(x, g.reshape(1, D))


def _inproj_kernel(h_ref, wqkvf_ref, wum_ref, bf_ref, q_ref, k_ref, v_ref, kb_ref, vb_ref, u_ref, qm_ref, lf_ref,
                   *, wa, wb, wm, nh, q_scale):
    h = h_ref[...]
    scale = HEAD_DIM ** -0.5
    n_seq, _, seq_rows, _ = q_ref.shape

    def proj(a, b, w_ref=wqkvf_ref):
        return _dot(h, w_ref[:, a:b])

    def put_heads(ref, val):
        for s in range(n_seq):
            for hd in range(nh):
                ref[s, hd] = val[s * seq_rows:(s + 1) * seq_rows, hd * HEAD_DIM:(hd + 1) * HEAD_DIM]

    put_heads(q_ref, (proj(0, wa) * q_scale).astype(BF16))
    k = proj(wa, 2 * wa)
    put_heads(k_ref, k)
    put_heads(kb_ref, k.astype(BF16))
    v = proj(2 * wa, 3 * wa)
    put_heads(v_ref, v)
    put_heads(vb_ref, v.astype(BF16))
    o = 3 * wa
    fl = proj(o, o + LANES) + bf_ref[...]
    lf_ref[...] = _log_sigmoid(fl)[:, :nh]
    u_ref[...] = proj(0, wb, wum_ref)
    qm_ref[...] = (proj(wb, wb + wm, wum_ref) * scale).astype(BF16)


def _inproj_fill_kernel(h_ref, wqkvf_ref, wum_ref, bf_ref, kall_ref, vall_ref, *out_refs, **kw):
    del kall_ref, vall_ref
    _inproj_kernel(h_ref, wqkvf_ref, wum_ref, bf_ref, *out_refs, **kw)


def inproj(h, w_qkvf, w_um, layer, bf, wa, wb, wm, nh, n_seq, q_scale, kv_all):
    M, D = h.shape
    depth = w_qkvf.shape[0]
    L = M // n_seq
    bm = _tile(L, 512) if L >= SUBLANES else M
    tiles_per_seq = max(L // bm, 1)
    seqs_per_tile = max(bm // L, 1)
    rows_per_seq = min(bm, L)
    row = lambda n: pl.BlockSpec((bm, n), lambda i: (i, 0))
    heads = pl.BlockSpec((seqs_per_tile, nh, rows_per_seq, HEAD_DIM),
                         lambda i: (i // tiles_per_seq, 0, i % tiles_per_seq, 0))
    slab = pl.BlockSpec((None, seqs_per_tile, nh, rows_per_seq, HEAD_DIM),
                        lambda i: (layer, i // tiles_per_seq, 0, i % tiles_per_seq, 0))
    hshape = lambda dt: jax.ShapeDtypeStruct((n_seq, nh, L, HEAD_DIM), dt)
    all_shape = jax.ShapeDtypeStruct((depth, n_seq, nh, L, HEAD_DIM), F32)
    in_specs = [row(D), _resident((None, D, w_qkvf.shape[2]), lambda i: (layer, 0, 0)),
                _resident((None, D, w_um.shape[2]), lambda i: (layer, 0, 0)),
                pl.BlockSpec((1, LANES), lambda i: (0, 0))] + [pl.BlockSpec(memory_space=pl.ANY)] * 2
    return pl.pallas_call(
        functools.partial(_inproj_fill_kernel, wa=wa, wb=wb, wm=wm, nh=nh, q_scale=q_scale),
        grid=(M // bm,), in_specs=in_specs,
        out_specs=[heads, slab, slab, heads, heads, row(wb), row(wm), row(nh)],
        out_shape=[hshape(BF16), all_shape, all_shape, hshape(BF16), hshape(BF16),
                   jax.ShapeDtypeStruct((M, wb), F32), jax.ShapeDtypeStruct((M, wm), BF16),
                   jax.ShapeDtypeStruct((M, nh), F32)],
        input_output_aliases={4: 1, 5: 2},
        compiler_params=_params(("parallel",)), name="inproj",
    )(h, w_qkvf, w_um, bf, *kv_all)


def _fox_bias_kernel(x_ref, qa_ref, ka_ref):
    x = x_ref[0]
    L = x.shape[-1]
    nh = qa_ref.shape[1]
    lane = lax.broadcasted_iota(jnp.int32, x.shape, 1)
    s = 1
    while s < L:
        x = x + jnp.where(lane >= s, pltpu.roll(x, s, axis=1), 0.0)
        s *= 2
    t = jnp.transpose(jnp.concatenate([x * LOG2E, jnp.zeros((LANES - x.shape[0], L), F32)], axis=0))

    col = lax.broadcasted_iota(jnp.int32, t.shape, 1)
    for h in range(nh):
        c = jnp.broadcast_to(t[:, h:h + 1], t.shape)
        hi = _top_half(c)
        mid = _top_half(c - hi)
        lo = c - hi - mid
        ones = lambda a, b: jnp.where((col >= a) & (col < b), 1.0, 0.0)
        qa = jnp.where(col == 0, hi, jnp.where(col == 1, mid, jnp.where(col == 2, lo, ones(3, 6))))
        ka = jnp.where(col == 3, -hi, jnp.where(col == 4, -mid, jnp.where(col == 5, -lo, ones(0, 3))))
        qa_ref[0, h] = qa.astype(BF16)
        ka_ref[0, h] = ka.astype(BF16)


def fox_bias_features(logf, nh):
    B, HP, L = logf.shape
    out = pl.BlockSpec((1, nh, L, HEAD_DIM), lambda b: (b, 0, 0, 0))
    return pl.pallas_call(
        _fox_bias_kernel, grid=(B,),
        in_specs=[pl.BlockSpec((1, HP, L), lambda b: (b, 0, 0))],
        out_specs=[out, out],
        out_shape=[jax.ShapeDtypeStruct((B, nh, L, HEAD_DIM), BF16)] * 2,
        compiler_params=_params(("parallel",)), name="fox_bias",
    )(logf)


def _fox_kernel(qt_ref, kt_ref, q_ref, qa_ref, k_ref, ka_ref, v_ref, o_ref, m_sc, l_sc, acc_sc):
    qi = qt_ref[pl.program_id(1)]
    ki = kt_ref[pl.program_id(1)]
    nh = q_ref.shape[0]

    @pl.when(ki == 0)
    def _():
        m_sc[...] = jnp.full(m_sc.shape, -jnp.inf, F32)
        l_sc[...] = jnp.zeros(l_sc.shape, F32)
        acc_sc[...] = jnp.zeros(acc_sc.shape, F32)

    def update(masked):
        heads = range(nh)
        s = [_dot_nt(jnp.concatenate([q_ref[h], qa_ref[h]], axis=1),
                     jnp.concatenate([k_ref[h], ka_ref[h]], axis=1)) for h in heads]
        if masked:
            row = lax.broadcasted_iota(jnp.int32, s[0].shape, 0)
            col = lax.broadcasted_iota(jnp.int32, s[0].shape, 1)
            s = [jnp.where(col <= row, sh, -jnp.inf) for sh in s]
        m_prev = [m_sc[h] for h in heads]
        m_new = [jnp.maximum(m_prev[h], jnp.max(s[h], axis=-1, keepdims=True)) for h in heads]
        alpha = [jnp.exp2(m_prev[h] - m_new[h]) for h in heads]
        p = [jnp.exp2(s[h] - m_new[h]) for h in heads]
        for h in heads:
            l_sc[h] = alpha[h] * l_sc[h] + jnp.sum(p[h], axis=-1, keepdims=True)
            m_sc[h] = m_new[h]
        pv = [_dot(p[h].astype(BF16), v_ref[h]) for h in heads]
        for h in heads:
            acc_sc[h] = alpha[h] * acc_sc[h] + pv[h]

    @pl.when(ki < qi)
    def _():
        update(False)

    @pl.when(ki == qi)
    def _():
        update(True)
        for h in range(nh):
            o_ref[:, h * HEAD_DIM:(h + 1) * HEAD_DIM] = (acc_sc[h] / l_sc[h]).astype(o_ref.dtype)


def fox_prompt(q, qa, k, ka, v):
    B, nh, L, _ = q.shape
    bq = _tile(L, 512)
    nq = L // bq
    pairs = [(qi, ki) for qi in range(nq) for ki in range(qi + 1)]
    qt = jnp.array([p[0] for p in pairs], jnp.int32)
    kt = jnp.array([p[1] for p in pairs], jnp.int32)
    qspec = pl.BlockSpec((None, nh, bq, HEAD_DIM), lambda b, p, qt, kt: (b, 0, qt[p], 0))
    kspec = pl.BlockSpec((None, nh, bq, HEAD_DIM), lambda b, p, qt, kt: (b, 0, kt[p], 0))
    grid_spec = pltpu.PrefetchScalarGridSpec(
        num_scalar_prefetch=2, grid=(B, len(pairs)),
        in_specs=[qspec, qspec, kspec, kspec, kspec],
        out_specs=pl.BlockSpec((None, bq, nh * HEAD_DIM), lambda b, p, qt, kt: (b, qt[p], 0)),
        scratch_shapes=[pltpu.VMEM((nh, bq, 1), F32), pltpu.VMEM((nh, bq, 1), F32),
                        pltpu.VMEM((nh, bq, HEAD_DIM), F32)])
    return pl.pallas_call(
        _fox_kernel, grid_spec=grid_spec,
        out_shape=jax.ShapeDtypeStruct((B, L, nh * HEAD_DIM), BF16),
        compiler_params=_params(("parallel", "arbitrary")), name="fox_prompt",
    )(qt, kt, q, qa, k, ka, v)


def _fox_decode_kernel(pt_ref, q_ref, lx_ref, kn_ref, vn_ref, *refs, pps, n_new):
    kp = refs[:pps]
    vp = refs[pps:2 * pps]
    lp = refs[2 * pps:3 * pps]
    o_ref, m_sc, l_sc, acc_sc, suf_sc = refs[3 * pps:]
    del pt_ref
    step = pl.program_id(1)
    nh, tp, _ = q_ref.shape
    page = kn_ref.shape[1]

    @pl.when(step == 0)
    def _():
        m_sc[...] = jnp.full(m_sc.shape, -jnp.inf, F32)
        l_sc[...] = jnp.zeros(l_sc.shape, F32)
        acc_sc[...] = jnp.zeros(acc_sc.shape, F32)
        suf_sc[...] = jnp.zeros(suf_sc.shape, F32)

    ck = lx_ref[...]
    col = lax.broadcasted_iota(jnp.int32, ck.shape, 1)
    row_t = lax.broadcasted_iota(jnp.int32, ck.shape, 0) % tp
    s = 1
    while s < n_new:
        ck = ck + jnp.where(col >= s, pltpu.roll(ck, s, axis=1), 0.0)
        s *= 2
    cq = jnp.sum(jnp.where(col == row_t, ck, 0.0), axis=-1, keepdims=True)

    def update(sc, v_tiles):
        m_prev = m_sc[...]
        m_new = jnp.maximum(m_prev, jnp.max(sc, axis=-1, keepdims=True))
        alpha = jnp.exp(m_prev - m_new)
        p = jnp.exp(sc - m_new)
        l_sc[...] = alpha * l_sc[...] + jnp.sum(p, axis=-1, keepdims=True)
        pv = []
        for h in range(nh):
            ph = p[h * tp:(h + 1) * tp].astype(BF16)
            acc = _dot(ph[:, :page], v_tiles[0](h))
            for i in range(1, len(v_tiles)):
                acc += _dot(ph[:, i * page:(i + 1) * page], v_tiles[i](h))
            pv.append(acc)
        acc_sc[...] = alpha * acc_sc[...] + jnp.concatenate(pv, axis=0)
        m_sc[...] = m_new

    lane = lax.broadcasted_iota(jnp.int32, (HEADS_PAD, LANES), 1)
    after = suf_sc[...]
    biases = []
    for i in range(pps):
        lf = lp[i][...]
        inc = lf
        s = 1
        while s < LANES:
            inc = inc + jnp.where(lane < LANES - s, pltpu.roll(inc, LANES - s, axis=1), 0.0)
            s *= 2
        biases.append(inc - lf + after)
        after = after + jnp.broadcast_to(inc[:, 0:1], after.shape)
    suf_sc[...] = after
    bias = jnp.concatenate(biases, axis=1)

    rows = []
    for h in range(nh):
        qh = q_ref[h]
        sh = jnp.concatenate([_dot_nt(qh, kp[i][h].astype(BF16)) for i in range(pps)], axis=1)
        rows.append(sh + bias[h:h + 1])
    update(jnp.concatenate(rows, axis=0) + cq,
           [lambda h, i=i: vp[i][h].astype(BF16) for i in range(pps)])

    @pl.when(step == pl.num_programs(1) - 1)
    def _():
        sc = jnp.concatenate([_dot_nt(q_ref[h], kn_ref[h]) for h in range(nh)], axis=0) + cq - ck
        sc = jnp.where((col <= row_t) & (col < n_new), sc, -jnp.inf)
        update(sc, [lambda h: vn_ref[h]])
        o_ref[...] = acc_sc[...] / l_sc[...]


def fox_decode(layer, page_table, q, lx, kn, vn, cache_k, cache_v, cache_lf, n_new):
    Bd, nh, tp, _ = q.shape
    n_pages = page_table.shape[1]
    page = cache_k.shape[3]
    pps = DECODE_PAGES_PER_STEP if n_pages % DECODE_PAGES_PER_STEP == 0 else 1
    steps = n_pages // pps
    R = nh * tp

    def page_spec(i, shape):
        zeros = (0,) * len(shape)
        return pl.BlockSpec((None, None) + shape,
                            lambda b, s, pt: (layer, pt[b, n_pages - 1 - (s * pps + i)]) + zeros)

    def per_b(shape):
        zeros = (0,) * len(shape)
        return pl.BlockSpec((None,) + shape, lambda b, s, pt: (b,) + zeros)

    in_specs = ([per_b((nh, tp, HEAD_DIM)), per_b((R, LANES)), per_b((nh, page, HEAD_DIM)),
                 per_b((nh, page, HEAD_DIM))]
                + [page_spec(i, (nh, page, HEAD_DIM)) for i in range(pps)]
                + [page_spec(i, (nh, page, HEAD_DIM)) for i in range(pps)]
                + [page_spec(i, (HEADS_PAD, page)) for i in range(pps)])
    grid_spec = pltpu.PrefetchScalarGridSpec(
        num_scalar_prefetch=1, grid=(Bd, steps), in_specs=in_specs,
        out_specs=per_b((R, HEAD_DIM)),
        scratch_shapes=[pltpu.VMEM((R, 1), F32), pltpu.VMEM((R, 1), F32), pltpu.VMEM((R, HEAD_DIM), F32),
                        pltpu.VMEM((HEADS_PAD, LANES), F32)])
    return pl.pallas_call(
        functools.partial(_fox_decode_kernel, pps=pps, n_new=n_new), grid_spec=grid_spec,
        out_shape=jax.ShapeDtypeStruct((Bd, R, HEAD_DIM), F32),
        compiler_params=_params(("parallel", "arbitrary")), name="fox_decode",
    )(page_table, q, lx, kn, vn, *([cache_k] * pps), *([cache_v] * pps), *([cache_lf] * pps))


def _s5_disc_kernel(are_ref, aim_ref, ldt_ref, bre_ref, bim_ref, abre_ref, abim_ref, bbre_ref, bbim_ref):
    a_re = are_ref[0]
    a_im = aim_ref[0]
    dt = jnp.exp(ldt_ref[0])
    mag = jnp.exp(a_re * dt)
    ab_re = mag * jnp.cos(a_im * dt)
    ab_im = mag * jnp.sin(a_im * dt)
    den = a_re * a_re + a_im * a_im
    n_re = ab_re - 1.0
    q_re = (n_re * a_re + ab_im * a_im) / den
    q_im = (ab_im * a_re - n_re * a_im) / den
    b_re = bre_ref[0]
    b_im = bim_ref[0]
    abre_ref[0] = ab_re
    abim_ref[0] = ab_im
    bbre_ref[0] = q_re * b_re - q_im * b_im
    bbim_ref[0] = q_re * b_im + q_im * b_re


def s5_discretize(a_re, a_im, log_dt, b_re_t, b_im_t):
    Dp, G, P = a_re.shape
    C = b_re_t.shape[2]
    gp = pl.BlockSpec((1, G, 1, P), lambda l: (l, 0, 0, 0))
    gcp = pl.BlockSpec((1, G, C, P), lambda l: (l, 0, 0, 0))
    return pl.pallas_call(
        _s5_disc_kernel, grid=(Dp,),
        in_specs=[gp, gp, pl.BlockSpec((1, G, 1, 1), lambda l: (l, 0, 0, 0)), gcp, gcp],
        out_specs=[gp, gp, gcp, gcp],
        out_shape=[jax.ShapeDtypeStruct((Dp, G, 1, P), F32)] * 2 + [jax.ShapeDtypeStruct((Dp, G, C, P), F32)] * 2,
        compiler_params=_params(("parallel",)), name="s5_discretize",
    )(a_re.reshape(Dp, G, 1, P), a_im.reshape(Dp, G, 1, P), log_dt.reshape(Dp, G, 1, 1), b_re_t, b_im_t)


def _s5_kernel(u_ref, h0re_ref, h0im_ref, are_ref, aim_ref, wbu_ref, wcre_ref, wcim_ref, d_ref, wglu_ref,
               o_ref, hre_ref, him_ref, bre_sc, bim_sc, *, nb, lane_chunk, batch_major):
    n = pl.program_id(0)
    rows = bre_sc.shape[0]
    tc = rows // nb
    gp = bre_sc.shape[1]
    kc = wbu_ref.shape[1]
    sc = wbu_ref.shape[2] // 2
    n_chunks = wbu_ref.shape[0]

    @pl.when(n == 0)
    def _():
        hre_ref[...] = h0re_ref[...]
        him_ref[...] = h0im_ref[...]

    if batch_major:
        r_i = lax.broadcasted_iota(jnp.int32, (rows, tc), 0)
        t_i = lax.broadcasted_iota(jnp.int32, (rows, tc), 1)
        spread, terms = [], []
        for b in range(nb):
            x = u_ref[b]
            hi = _top_half(x)
            mid = _top_half(x - hi)
            terms += [hi.astype(BF16), mid.astype(BF16), (x - hi - mid).astype(BF16)]
            spread += [(r_i == t_i * nb + b).astype(BF16)] * 3
        u = _dot(jnp.concatenate(spread, axis=1), jnp.concatenate(terms, axis=0))
    else:
        u = u_ref[...]
    ub = u.astype(BF16)
    for c in range(n_chunks):
        bu = _dot(ub[:, c * kc:(c + 1) * kc], wbu_ref[c])
        bre_sc[:, c * sc:(c + 1) * sc] = bu[:, :sc]
        bim_sc[:, c * sc:(c + 1) * sc] = bu[:, sc:]

    spt = SUBLANES // nb
    for c in range(gp // lane_chunk):
        cs = slice(c * lane_chunk, (c + 1) * lane_chunk)
        ar = jnp.broadcast_to(are_ref[:, cs], (SUBLANES, lane_chunk))
        ai = jnp.broadcast_to(aim_ref[:, cs], (SUBLANES, lane_chunk))
        band = lax.broadcasted_iota(jnp.int32, (SUBLANES, lane_chunk), 0) // nb

        def step(t, carry, cs=cs, ar=ar, ai=ai, band=band):
            hr, hi = carry
            r0 = pl.multiple_of(t * SUBLANES, SUBLANES)
            xr = bre_sc[pl.ds(r0, SUBLANES), cs]
            xi = bim_sc[pl.ds(r0, SUBLANES), cs]
            out_r = out_i = None
            for k in range(spt):
                hr, hi = ar * hr - ai * hi + xr, ar * hi + ai * hr + xi
                out_r = hr if k == 0 else jnp.where(band == k, hr, out_r)
                out_i = hi if k == 0 else jnp.where(band == k, hi, out_i)
                if spt > 1:
                    hr = pltpu.roll(hr, nb, axis=0)
                    hi = pltpu.roll(hi, nb, axis=0)
            bre_sc[pl.ds(r0, SUBLANES), cs] = out_r
            bim_sc[pl.ds(r0, SUBLANES), cs] = out_i
            return hr, hi

        init = lambda ref: jnp.concatenate([ref[:, cs]] * spt, axis=0)
        hr, hi = lax.fori_loop(0, rows // SUBLANES, step, (init(hre_ref), init(him_ref)))
        hre_ref[:, cs] = hr[:nb]
        him_ref[:, cs] = hi[:nb]

    ys = []
    for c in range(n_chunks):
        hr = bre_sc[:, c * sc:(c + 1) * sc].astype(BF16)
        hi = bim_sc[:, c * sc:(c + 1) * sc].astype(BF16)
        ys.append(_dot(hr, wcre_ref[c]) - _dot(hi, wcim_ref[c]))
    y = jnp.concatenate(ys, axis=1) + d_ref[...] * u
    z = _gelu_tanh(y)
    o = (z * _sigmoid(_dot(z.astype(BF16), wglu_ref[...]))).astype(BF16)
    if batch_major:
        i_i = lax.broadcasted_iota(jnp.int32, (rows, rows), 0)
        r_i = lax.broadcasted_iota(jnp.int32, (rows, rows), 1)
        gather = (r_i == (i_i % tc) * nb + i_i // tc).astype(BF16)
        ob = _dot(gather, o).astype(BF16)
        for b in range(nb):
            o_ref[b] = ob[b * tc:(b + 1) * tc]
    else:
        o_ref[...] = o


def s5_branch(u, h0_re, h0_im, ab_re, ab_im, wbu, wc_re, wc_im, d_skip, w_glu, nb):
    batch_major = u.ndim == 3
    L = u.shape[1] if batch_major else u.shape[0] // nb
    wb = u.shape[-1]
    gp = h0_re.shape[1]
    tc = _tile(L, 128)
    rows = tc * nb
    lane_chunk = _tile(gp, 768)
    full = lambda a: _resident(a.shape, lambda n: (0,) * a.ndim)
    st = pl.BlockSpec((nb, gp), lambda n: (0, 0))
    io = (pl.BlockSpec((nb, tc, wb), lambda n: (0, n, 0)) if batch_major
          else pl.BlockSpec((rows, wb), lambda n: (n, 0)))
    return pl.pallas_call(
        functools.partial(_s5_kernel, nb=nb, lane_chunk=lane_chunk, batch_major=batch_major), grid=(L // tc,),
        in_specs=[io, st, st, full(ab_re), full(ab_im),
                  full(wbu), full(wc_re), full(wc_im), full(d_skip), full(w_glu)],
        out_specs=[io, st, st],
        out_shape=[jax.ShapeDtypeStruct(u.shape, BF16), jax.ShapeDtypeStruct((nb, gp), F32),
                   jax.ShapeDtypeStruct((nb, gp), F32)],
        scratch_shapes=[pltpu.VMEM((rows, gp), F32), pltpu.VMEM((rows, gp), F32)],
        compiler_params=_params(("arbitrary",)), name="s5_branch",
    )(u, h0_re, h0_im, ab_re, ab_im, wbu, wc_re, wc_im, d_skip, w_glu)


def _mem_kv_kernel(x_ref, g_ref, w_ref, k_ref, v_ref, kb_ref, vb_ref):
    h = _rms(x_ref[...], g_ref[0]).astype(BF16)
    z = _dot(h, w_ref[0])
    wm = k_ref.shape[-1]
    k_ref[0] = z[:, :wm]
    v_ref[0] = z[:, wm:]
    kb_ref[0] = z[:, :wm].astype(BF16)
    vb_ref[0] = z[:, wm:].astype(BF16)


def mem_kv(mem, g_mem, w_mkv):
    R, D = mem.shape
    Dp, _, W2 = w_mkv.shape
    wm = W2 // 2
    bm = _tile(R, 512)
    o = pl.BlockSpec((1, bm, wm), lambda l, i: (l, i, 0))
    return pl.pallas_call(
        _mem_kv_kernel, grid=(Dp, R // bm),
        in_specs=[pl.BlockSpec((bm, D), lambda l, i: (i, 0)), pl.BlockSpec((1, 1, D), lambda l, i: (l, 0, 0)),
                  pl.BlockSpec((1, D, W2), lambda l, i: (l, 0, 0))],
        out_specs=[o, o, o, o],
        out_shape=[jax.ShapeDtypeStruct((Dp, R, wm), F32)] * 2 + [jax.ShapeDtypeStruct((Dp, R, wm), BF16)] * 2,
        compiler_params=_params(("parallel", "arbitrary")), name="mem_kv",
    )(mem, g_mem, w_mkv)


def _mem_attn_kernel(q_ref, k_ref, v_ref, o_ref):
    nh = q_ref.shape[-1] // HEAD_DIM
    for h in range(nh):
        hs = slice(h * HEAD_DIM, (h + 1) * HEAD_DIM)
        s = _dot_nt(q_ref[:, hs], k_ref[:, hs])
        p = jnp.exp(s - jnp.max(s, axis=-1, keepdims=True))
        p = p / jnp.sum(p, axis=-1, keepdims=True)
        o_ref[:, hs] = _dot(p.astype(BF16), v_ref[:, hs]).astype(o_ref.dtype)


def mem_attn(q, k, v):
    B, L, W = q.shape
    Nm = k.shape[1]
    bq = _tile(L, 512)
    qs = pl.BlockSpec((None, bq, W), lambda b, i: (b, i, 0))
    ks = pl.BlockSpec((None, Nm, W), lambda b, i: (b, 0, 0))
    return pl.pallas_call(
        _mem_attn_kernel, grid=(B, L // bq), in_specs=[qs, ks, ks], out_specs=qs,
        out_shape=jax.ShapeDtypeStruct((B, L, W), BF16),
        compiler_params=_params(("parallel", "parallel")), name="mem_attn",
    )(q, k, v)


def _merge_kernel(h_ref, oa_ref, ob_ref, om_ref, g0_ref, g1_ref, g2_ref, pa_ref, pb_ref, pm_ref, o_ref):
    h = h_ref[...]
    acc = _sigmoid(_dot(h, g0_ref[...])) * _dot(oa_ref[...], pa_ref[...])
    acc += _sigmoid(_dot(h, g1_ref[...])) * _dot(ob_ref[...], pb_ref[...])
    acc += _sigmoid(_dot(h, g2_ref[...])) * _dot(om_ref[...], pm_ref[...])
    o_ref[...] = acc.astype(o_ref.dtype)


def merge(h, oa, ob, om, w, layer, pa, pb, pm):
    M, D = h.shape
    bm = _tile(M, 1024)
    bn = _tile(D, 512)
    nj = D // bn
    rowspec = lambda a: pl.BlockSpec((bm, a.shape[1]), lambda j, i: (i, 0))
    colspec = lambda a: pl.BlockSpec((a.shape[0], bn), lambda j, i: (0, j))
    gate = lambda b: pl.BlockSpec((None, D, bn), lambda j, i: (layer, 0, b * nj + j))
    return pl.pallas_call(
        _merge_kernel, grid=(nj, M // bm),
        in_specs=[rowspec(h), rowspec(oa), rowspec(ob), rowspec(om), gate(0), gate(1), gate(2),
                  colspec(pa), colspec(pb), colspec(pm)],
        out_specs=pl.BlockSpec((bm, bn), lambda j, i: (i, j)),
        out_shape=jax.ShapeDtypeStruct((M, D), BF16),
        compiler_params=_params(("parallel", "arbitrary")), name="merge",
    )(h, oa, ob, om, w, w, w, pa, pb, pm)


def _proj_res_kernel(a_ref, w_ref, x_ref, gpost_ref, gnext_ref, xo_ref, *h_ref):
    f = _dot(a_ref[...], w_ref[...])
    x = x_ref[...] + _rms(f, gpost_ref[...])
    xo_ref[...] = x
    if h_ref:
        h_ref[0][...] = _rms(x, gnext_ref[...]).astype(BF16)


def proj_residual(a, w, x, g_post, g_next, emit_h=True):
    M, K = a.shape
    D = w.shape[1]
    bm = _tile(M, 256)
    row = lambda n: pl.BlockSpec((bm, n), lambda i: (i, 0))
    g = pl.BlockSpec((1, D), lambda i: (0, 0))
    out_specs = [row(D)] + ([row(D)] if emit_h else [])
    out_shape = [jax.ShapeDtypeStruct((M, D), F32)] + ([jax.ShapeDtypeStruct((M, D), BF16)] if emit_h else [])
    res = pl.pallas_call(
        _proj_res_kernel, grid=(M // bm,),
        in_specs=[row(K), _resident((K, D), lambda i: (0, 0)), row(D), g, g],
        out_specs=out_specs, out_shape=out_shape,
        compiler_params=_params(("parallel",)), name="proj_residual",
    )(a, w, x, g_post.reshape(1, D), g_next.reshape(1, D))
    return (res[0], res[1]) if emit_h else (res[0], None)


def _ffn_up_kernel(h_ref, hd_ref, wa_ref, wb_ref, sta_ref, stb_ref, cwa_ref, cwb_ref, cba_ref, cbb_ref,
                   act_ref, sa_ref, sb_ref, actd_ref, sad_ref, sbd_ref, wa_sc, wb_sc, ca_sc, cb_sc,
                   *, tiles_per_seq, n_dec):
    i = pl.program_id(1)
    bm = h_ref.shape[0]

    def taps(u, u1, u2, cw_ref, cb_ref):
        return cb_ref[...] + cw_ref[2:3] * u + cw_ref[1:2] * u1 + cw_ref[0:1] * u2

    @pl.when(i == 0)
    def _():
        wa_sc[...] = wa_ref[...].astype(BF16)
        wb_sc[...] = wb_ref[...].astype(BF16)
        hd = hd_ref[...]
        n_new = hd.shape[0]

        def conv_dec(w_sc, st_ref, cw_ref, cb_ref, s_ref):
            u = _dot(hd, w_sc[...])
            ext = jnp.concatenate([st_ref[...], u], axis=0)
            s_ref[...] = ext[n_new:]
            return taps(u, ext[n_dec:n_dec + n_new], ext[:n_new], cw_ref, cb_ref)

        ya = conv_dec(wa_sc, sta_ref, cwa_ref, cba_ref, sad_ref)
        yb = conv_dec(wb_sc, stb_ref, cwb_ref, cbb_ref, sbd_ref)
        actd_ref[...] = (_gelu_tanh(ya) * yb).astype(actd_ref.dtype)

    @pl.when(i % tiles_per_seq == 0)
    def _():
        ca_sc[...] = jnp.zeros(ca_sc.shape, F32)
        cb_sc[...] = jnp.zeros(cb_sc.shape, F32)

    def conv(w_sc, c_sc, cw_ref, cb_ref, s_ref):
        u = _dot(h_ref[...], w_sc[...])
        prev = c_sc[...]
        row = lax.broadcasted_iota(jnp.int32, u.shape, 0)
        u1 = jnp.where(row == 0, prev[SUBLANES - 1:SUBLANES], pltpu.roll(u, 1, axis=0))
        u2 = jnp.where(row == 0, prev[SUBLANES - 2:SUBLANES - 1],
                       jnp.where(row == 1, prev[SUBLANES - 1:SUBLANES], pltpu.roll(u, 2, axis=0)))
        c_sc[...] = u[bm - SUBLANES:]
        s_ref[...] = u[bm - 2:]
        return taps(u, u1, u2, cw_ref, cb_ref)

    ya = conv(wa_sc, ca_sc, cwa_ref, cba_ref, sa_ref)
    yb = conv(wb_sc, cb_sc, cwb_ref, cbb_ref, sb_ref)
    act_ref[...] = (_gelu_tanh(ya) * yb).astype(act_ref.dtype)


def ffn_up(h, h_dec, w_up, layer, conv_w, conv_b, state_dec, n_seq, n_dec):
    M, D = h.shape
    Md = h_dec.shape[0]
    F2 = w_up.shape[2]
    F = F2 // 2
    L = M // n_seq
    bm = _tile(L, 1024)
    bn = _tile(F, 512)
    nj = F // bn
    tps = L // bm
    a_col = lambda r: pl.BlockSpec((r, bn), lambda j, i: (0, j))
    b_col = lambda r: pl.BlockSpec((r, bn), lambda j, i: (0, nj + j))
    wa = pl.BlockSpec((None, D, bn), lambda j, i: (layer, 0, j))
    wb = pl.BlockSpec((None, D, bn), lambda j, i: (layer, 0, nj + j))
    st = pl.BlockSpec((None, 2, bn), lambda j, i: (i // tps, 0, j))
    return pl.pallas_call(
        functools.partial(_ffn_up_kernel, tiles_per_seq=tps, n_dec=n_dec), grid=(nj, M // bm),
        in_specs=[pl.BlockSpec((bm, D), lambda j, i: (i, 0)), pl.BlockSpec((Md, D), lambda j, i: (0, 0)), wa, wb,
                  a_col(2 * n_dec), b_col(2 * n_dec), a_col(3), b_col(3), a_col(1), b_col(1)],
        out_specs=[pl.BlockSpec((bm, bn), lambda j, i: (i, j)), st, st,
                   a_col(Md), a_col(2 * n_dec), a_col(2 * n_dec)],
        out_shape=[jax.ShapeDtypeStruct((M, F), BF16), jax.ShapeDtypeStruct((n_seq, 2, F), F32),
                   jax.ShapeDtypeStruct((n_seq, 2, F), F32), jax.ShapeDtypeStruct((Md, F), BF16),
                   jax.ShapeDtypeStruct((2 * n_dec, F), F32), jax.ShapeDtypeStruct((2 * n_dec, F), F32)],
        scratch_shapes=[pltpu.VMEM((D, bn), BF16), pltpu.VMEM((D, bn), BF16),
                        pltpu.VMEM((SUBLANES, bn), F32), pltpu.VMEM((SUBLANES, bn), F32)],
        compiler_params=_params(("parallel", "arbitrary")), name="ffn_up",
    )(h, h_dec, w_up, w_up, state_dec, state_dec, conv_w, conv_w, conv_b, conv_b)


def _s5_weights(bb_re_t, bb_im_t, c_re, c_im):
    G, C, P = bb_re_t.shape
    gc = S5_GROUPS_PER_CHUNK if G % S5_GROUPS_PER_CHUNK == 0 else G
    n = G // gc
    eye = jnp.eye(gc, dtype=F32)

    def bu(bb):
        return jnp.einsum('ngcp,gh->ngchp', bb.reshape(n, gc, C, P), eye).reshape(n, gc * C, gc * P)

    def cy(c):
        return jnp.einsum('ngcp,gh->ngphc', c.reshape(n, gc, C, P), eye).reshape(n, gc * P, gc * C)

    wbu = jnp.concatenate([bu(bb_re_t), bu(bb_im_t)], axis=-1).astype(BF16)
    return wbu, cy(c_re).astype(BF16), cy(c_im).astype(BF16)


def _mixer_block(x, h, wl, n_seq, q_scale, kv_all, branch_fn):
    q, k, v, kb, vb, u, qm, lf = inproj(h, wl['w_qkvf'], wl['w_um'], wl['layer'], wl['b_f'], wl['wa'], wl['wb'],
                                        wl['wm'], wl['nh'], n_seq, q_scale, kv_all)
    o_a, o_b, o_m, extra = branch_fn(q, kb, vb, lf, u, qm)
    merged = merge(h, o_a, o_b, o_m, wl['w_gate'], wl['layer'], wl['p_a'], wl['p_b'], wl['p_m'])
    x, h2 = proj_residual(merged, wl['w_out'], x, wl['g_post_mix'], wl['g_pre_ffn'])
    return x, h2, k, v, lf, extra


def kernel(x_prompt, x_sample, cache_k, cache_v, cache_logf, cache_mem_k, cache_mem_v, state_ssm_re, state_ssm_im, state_conv, page_table, mem_prompt, w_in, b_f, a_re, a_im, log_dt, b_re, b_im, c_re, c_im, d_skip, w_glu, g_mem, w_mkv, p_a, p_b, p_m, w_out, g_pre_mix, g_post_mix, g_pre_ffn, g_post_ffn, w_up, conv_w, conv_b, w_down):
    Bp, L, D = x_prompt.shape
    Bd, T, _ = x_sample.shape
    depth = w_in.shape[0]
    wa = p_a.shape[1]
    wb = p_b.shape[1]
    wm = p_m.shape[1]
    nh = b_f.shape[1]
    nhm = wm // HEAD_DIM
    G, P = a_re.shape[1:]
    C = b_re.shape[3]
    gp = G * P
    n_mem = mem_prompt.shape[1]
    n_pool, page = cache_k.shape[1:3]
    F2 = w_up.shape[2]

    o_u = 3 * wa + nh
    o_g = o_u + wb + wm
    w_qkvf = jnp.pad(w_in[:, :, :o_u], ((0, 0), (0, 0), (0, LANES - nh))).astype(BF16)
    w_um = w_in[:, :, o_u:o_g].astype(BF16)
    w_gate = w_in[:, :, o_g:].astype(BF16)
    b_f_pad = jnp.pad(b_f, ((0, 0), (0, LANES - nh))).reshape(depth, 1, LANES)
    ab_re, ab_im, bb_re_t, bb_im_t = s5_discretize(a_re, a_im, log_dt, b_re.transpose(0, 1, 3, 2),
                                                   b_im.transpose(0, 1, 3, 2))
    bf = lambda a: a.astype(BF16)
    w_mkv_b, p_a_b, p_b_b, p_m_b, w_out_b, w_down_b, w_glu_b = map(
        bf, (w_mkv, p_a, p_b, p_m, w_out, w_down, w_glu))

    mk, mv, mkb, mvb = mem_kv(mem_prompt.reshape(Bp * n_mem, D), g_mem.reshape(depth, 1, D), w_mkv_b)

    cache_k2 = cache_k.transpose(0, 1, 3, 2, 4)
    cache_v2 = cache_v.transpose(0, 1, 3, 2, 4)
    cache_lf_t = jnp.pad(cache_logf.transpose(0, 1, 3, 2), ((0, 0), (0, 0), (0, HEADS_PAD - nh), (0, 0)))
    cache_mk_b = bf(cache_mem_k).reshape(depth, Bd, n_mem, wm)
    cache_mv_b = bf(cache_mem_v).reshape(depth, Bd, n_mem, wm)

    xp = x_prompt.reshape(Bp * L, D)
    xs = x_sample.reshape(Bd * T, D)
    hp = rmsnorm_bf16(xp, g_pre_mix[0])
    hs = rmsnorm_bf16(xs, g_pre_mix[0])
    zeros_p = jnp.zeros((Bp, gp), F32)

    outs = [[] for _ in range(14)]
    kv_p = tuple(jnp.zeros((depth, Bp, nh, L, HEAD_DIM), F32) for _ in range(2))
    kv_s = tuple(jnp.zeros((depth, Bd, nh, T, HEAD_DIM), F32) for _ in range(2))
    for l in range(depth):
        wbu, wc_re, wc_im = _s5_weights(bb_re_t[l], bb_im_t[l], c_re[l], c_im[l])
        wl = dict(w_qkvf=w_qkvf, w_um=w_um, w_gate=w_gate, layer=l, b_f=b_f_pad[l], wa=wa, wb=wb, wm=wm, nh=nh,
                  p_a=p_a_b[l], p_b=p_b_b[l], p_m=p_m_b[l], w_out=w_out_b[l], w_down=w_down_b[l],
                  g_post_mix=g_post_mix[l], g_pre_ffn=g_pre_ffn[l], g_post_ffn=g_post_ffn[l],
                  g_next=g_pre_mix[(l + 1) % depth])
        s5_args = (ab_re[l].reshape(1, gp), ab_im[l].reshape(1, gp), wbu, wc_re, wc_im,
                   d_skip[l].reshape(1, wb), w_glu_b[l])
        emit_h = l + 1 < depth

        def prompt_branches(q, kb, vb, lf, u, qm):
            lf_t = jnp.pad(lf.reshape(Bp, L, nh).transpose(0, 2, 1), ((0, 0), (0, HEADS_PAD - nh), (0, 0)))
            qa, ka = fox_bias_features(lf_t, nh)
            o_a = fox_prompt(q, qa, kb, ka, vb).reshape(Bp * L, wa)
            o_b, hre, him = s5_branch(u.reshape(Bp, L, wb), zeros_p, zeros_p, *s5_args, nb=Bp)
            o_b = o_b.reshape(Bp * L, wb)
            o_m = mem_attn(qm.reshape(Bp, L, wm), mkb[l].reshape(Bp, n_mem, wm),
                           mvb[l].reshape(Bp, n_mem, wm)).reshape(Bp * L, wm)
            return o_a, o_b, o_m, (hre, him)

        xp, h2p, k, v, lf, (hre, him) = _mixer_block(xp, hp, wl, Bp, HEAD_DIM ** -0.5 * LOG2E, kv_p,
                                                      prompt_branches)
        kv_p = (k, v)
        for i, a in zip((2, 3, 4, 5, 6), (
                lf.reshape(Bp, L, nh),
                mk[l].reshape(Bp, n_mem, nhm, HEAD_DIM), mv[l].reshape(Bp, n_mem, nhm, HEAD_DIM),
                hre.reshape(Bp, G, P), him.reshape(Bp, G, P))):
            outs[i].append(a)

        def sample_branches(q, kb, vb, lf, u, qm):
            tp = -(-T // SUBLANES) * SUBLANES
            padrows = lambda a, n: jnp.pad(a, ((0, 0), (0, 0), (0, n - T), (0, 0)))
            lx = jnp.pad(lf.reshape(Bd, T, nh).transpose(0, 2, 1), ((0, 0), (0, 0), (0, LANES - T)))
            lx = jnp.broadcast_to(lx[:, :, None], (Bd, nh, tp, LANES)).reshape(Bd, nh * tp, LANES)
            o = fox_decode(l, page_table, padrows(q, tp), lx, padrows(kb, page), padrows(vb, page),
                           cache_k2, cache_v2, cache_lf_t, T)
            o_a = o.reshape(Bd, nh, tp, HEAD_DIM)[:, :, :T].transpose(0, 2, 1, 3).reshape(Bd * T, wa).astype(BF16)
            u_tb = u.reshape(Bd, T, wb).transpose(1, 0, 2).reshape(T * Bd, wb)
            o_b, hre, him = s5_branch(u_tb, state_ssm_re[l].reshape(Bd, gp), state_ssm_im[l].reshape(Bd, gp),
                                      *s5_args, nb=Bd)
            o_b = o_b.reshape(T, Bd, wb).transpose(1, 0, 2).reshape(Bd * T, wb)
            o_m = mem_attn(qm.reshape(Bd, T, wm), cache_mk_b[l], cache_mv_b[l]).reshape(Bd * T, wm)
            return o_a, o_b, o_m, (hre, him)

        xs, h2s, k, v, lf, (hre, him) = _mixer_block(xs, hs, wl, Bd, HEAD_DIM ** -0.5, kv_s, sample_branches)
        kv_s = (k, v)
        for i, a in zip((10, 11, 12), (lf.reshape(Bd, T, nh), hre.reshape(Bd, G, P), him.reshape(Bd, G, P))):
            outs[i].append(a)

        h_tb = h2s.reshape(Bd, T, D).transpose(1, 0, 2).reshape(T * Bd, D)
        st_tb = state_conv[l].transpose(1, 0, 2).reshape(2 * Bd, F2)
        act_p, sa, sb, act_s, sad, sbd = ffn_up(h2p, h_tb, w_up, l, conv_w[l], conv_b[l].reshape(1, F2), st_tb, Bp, Bd)
        outs[7].append(jnp.concatenate([sa, sb], axis=-1))
        outs[13].append(jnp.concatenate([sad, sbd], axis=-1).reshape(2, Bd, F2).transpose(1, 0, 2))
        act_s = act_s.reshape(T, Bd, F2 // 2).transpose(1, 0, 2).reshape(Bd * T, F2 // 2)
        xp, hp = proj_residual(act_p, wl['w_down'], xp, wl['g_post_ffn'], wl['g_next'], emit_h=emit_h)
        xs, hs = proj_residual(act_s, wl['w_down'], xs, wl['g_post_ffn'], wl['g_next'], emit_h=emit_h)

    res = [jnp.stack(o) if o else None for o in outs]
    for i, a in zip((0, 1, 8, 9), kv_p + kv_s):
        res[i] = a.transpose(0, 1, 3, 2, 4)
    return (xp.reshape(Bp, L, D), xs.reshape(Bd, T, D)) + tuple(res)
```
